```python
import math
import jax, jax.numpy as jnp
from jax import lax
import numpy as np


D_MODEL = 1024
BATCH = 4
SEQ = 4096
DEPTH = 1

CHUNK = 64
QUERY_BLOCK = 2 * CHUNK
SB_HEADS = 8
SB_HEAD_DIM = 64
SB_WIDTH = SB_HEADS * SB_HEAD_DIM
CONV_GROUPS = 8
CONV_WIDTH = D_MODEL // 2
CONV_K = 3
N_BRANCHES = 2
IN_WIDTH = 3 * SB_WIDTH + 3 * CONV_WIDTH + N_BRANCHES * D_MODEL
PEER_HEADS = 8
PEER_QUERY_DIM = 256
PEER_HALF = PEER_QUERY_DIM // 2
N_KEYS = 128
N_EXPERTS = N_KEYS * N_KEYS
PEER_TOPK = 16
PEER_TOKEN_BLOCK = 128
RMS_EPS = 1e-6

kernel_name = 'hybrid_stickbreak_shortconv_peer'


def rms_norm(x, gain):
    x32 = x.astype(jnp.float32)
    y = x32 * lax.rsqrt(jnp.mean(x32 * x32, axis=-1, keepdims=True) + RMS_EPS)
    return (y * gain.astype(jnp.float32)).astype(x.dtype)


def stick_breaking_attention(q, k, v):
    b, h, t, dh = q.shape
    n_blk = t // QUERY_BLOCK
    scale = dh ** -0.5
    k32 = k.astype(jnp.float32)
    v32 = v.astype(jnp.float32)
    q_blocks = q.reshape(b, h, n_blk, QUERY_BLOCK, dh).transpose(2, 0, 1, 3, 4)
    key_pos = jnp.arange(t)

    def one_block(args):
        q_blk, blk = args
        z = jnp.einsum('bhqd,bhkd->bhqk', q_blk.astype(jnp.float32), k32) * scale
        q_pos = blk * QUERY_BLOCK + jnp.arange(QUERY_BLOCK)
        mask = key_pos[None, :] < q_pos[:, None]
        log_1m_beta = jnp.where(mask, jax.nn.log_sigmoid(-z), 0.0)
        stick = lax.cumsum(log_1m_beta, axis=3, reverse=True) - log_1m_beta
        att = jnp.where(mask, jnp.exp(jax.nn.log_sigmoid(z) + stick), 0.0)
        return jnp.einsum('bhqk,bhkd->bhqd', att, v32)

    out = lax.map(one_block, (q_blocks, jnp.arange(n_blk)))
    return out.transpose(1, 2, 0, 3, 4).reshape(b, h, t, dh).astype(q.dtype)


def short_conv_mixer(b_gate, c_gate, u, conv_w, conv_b):
    z = c_gate * u
    y = lax.conv_general_dilated(
        z, conv_w.astype(z.dtype), window_strides=(1,), padding=[(CONV_K - 1, 0)],
        dimension_numbers=('NWC', 'WIO', 'NWC'), feature_group_count=CONV_WIDTH)
    return b_gate * (y + conv_b.astype(z.dtype))


def peer_ffn(x, w_q, sub_keys, expert_u, expert_v):
    b, t, d = x.shape
    n_blk = (b * t) // PEER_TOKEN_BLOCK
    x_blocks = x.reshape(n_blk, PEER_TOKEN_BLOCK, d)
    keys32 = sub_keys.astype(jnp.float32)

    def one_block(x_blk):
        n = x_blk.shape[0]
        q = (x_blk @ w_q).reshape(n, PEER_HEADS, 2, PEER_HALF).astype(jnp.float32)
        s = jnp.einsum('nhpc,hpkc->nhpk', q, keys32)
        top_s, top_i = lax.top_k(s, PEER_TOPK)
        cand_s = top_s[:, :, 0, :, None] + top_s[:, :, 1, None, :]
        cand_i = top_i[:, :, 0, :, None] * N_KEYS + top_i[:, :, 1, None, :]
        cand_s = cand_s.reshape(n, PEER_HEADS, PEER_TOPK * PEER_TOPK)
        cand_i = cand_i.reshape(n, PEER_HEADS, PEER_TOPK * PEER_TOPK)
        best_s, best_c = lax.top_k(cand_s, PEER_TOPK)
        expert_idx = jnp.take_along_axis(cand_i, best_c, axis=-1)
        gate = jax.nn.softmax(best_s, axis=-1)
        u_sel = expert_u[expert_idx]
        v_sel = expert_v[expert_idx]
        act = jax.nn.gelu(jnp.einsum('nhkd,nd->nhk', u_sel, x_blk).astype(jnp.float32), approximate=False)
        w = (gate * act).astype(v_sel.dtype)
        return jnp.einsum('nhk,nhkd->nd', w, v_sel)

    return lax.map(one_block, x_blocks).reshape(b, t, d).astype(x.dtype)


def hybrid_layer(x, norm_mix, w_in, conv_w, conv_b, w_branch_a, w_branch_b, w_out,
                 norm_ffn, w_q, sub_keys, expert_u, expert_v):
    b, t, _ = x.shape
    h = rms_norm(x, norm_mix)
    proj = h @ w_in
    o1 = SB_WIDTH
    o2 = 2 * SB_WIDTH
    o3 = 3 * SB_WIDTH
    o4 = o3 + CONV_WIDTH
    o5 = o4 + CONV_WIDTH
    o6 = o5 + CONV_WIDTH
    q, k, v, cb, cc, cu, gate_logits = jnp.split(proj, [o1, o2, o3, o4, o5, o6], axis=-1)

    def heads(a):
        return a.reshape(b, t, SB_HEADS, SB_HEAD_DIM).transpose(0, 2, 1, 3)

    att = stick_breaking_attention(heads(q), heads(k), heads(v))
    y_a = att.transpose(0, 2, 1, 3).reshape(b, t, SB_WIDTH) @ w_branch_a
    y_b = short_conv_mixer(cb, cc, cu, conv_w, conv_b) @ w_branch_b

    g = jax.nn.sigmoid(gate_logits.astype(jnp.float32)).reshape(b, t, N_BRANCHES, D_MODEL)
    merged = (g[:, :, 0, :] * y_a.astype(jnp.float32) + g[:, :, 1, :] * y_b.astype(jnp.float32)).astype(x.dtype)
    x = x + merged @ w_out
    x = x + peer_ffn(rms_norm(x, norm_ffn), w_q, sub_keys, expert_u, expert_v)
    return x


def setup_inputs(seed: int = 0) -> dict:
    key = jax.random.key(seed)
    ks = jax.random.split(key, 16)
    f32 = jnp.float32
    nrm = lambda k, shape, s: jax.random.normal(k, shape, f32) * s
    return {
        'x': nrm(ks[0], (BATCH, SEQ, D_MODEL), 1.0),
        'norm_mix': 1.0 + nrm(ks[1], (DEPTH, D_MODEL), 0.02),
        'w_in': nrm(ks[2], (DEPTH, D_MODEL, IN_WIDTH), D_MODEL ** -0.5),
        'conv_w': nrm(ks[3], (DEPTH, CONV_K, 1, CONV_WIDTH), CONV_K ** -0.5),
        'conv_b': nrm(ks[4], (DEPTH, CONV_WIDTH), 0.01),
        'w_branch_a': nrm(ks[5], (DEPTH, SB_WIDTH, D_MODEL), SB_WIDTH ** -0.5),
        'w_branch_b': nrm(ks[6], (DEPTH, CONV_WIDTH, D_MODEL), CONV_WIDTH ** -0.5),
        'w_out': nrm(ks[7], (DEPTH, D_MODEL, D_MODEL), D_MODEL ** -0.5),
        'norm_ffn': 1.0 + nrm(ks[8], (DEPTH, D_MODEL), 0.02),
        'w_q': nrm(ks[9], (DEPTH, D_MODEL, PEER_HEADS * PEER_QUERY_DIM), D_MODEL ** -0.5),
        'sub_keys': nrm(ks[10], (DEPTH, PEER_HEADS, 2, N_KEYS, PEER_HALF), PEER_HALF ** -0.5),
        'expert_u': nrm(ks[11], (DEPTH, N_EXPERTS, D_MODEL), D_MODEL ** -0.5),
        'expert_v': nrm(ks[12], (DEPTH, N_EXPERTS, D_MODEL), PEER_HEADS ** -0.5),
        'final_norm': 1.0 + nrm(ks[13], (D_MODEL,), 0.02),
    }


def reference(x, norm_mix, w_in, conv_w, conv_b, w_branch_a, w_branch_b, w_out,
              norm_ffn, w_q, sub_keys, expert_u, expert_v, final_norm):
    for layer in range(DEPTH):
        x = hybrid_layer(x, norm_mix[layer], w_in[layer], conv_w[layer], conv_b[layer],
                         w_branch_a[layer], w_branch_b[layer], w_out[layer],
                         norm_ffn[layer], w_q[layer], sub_keys[layer],
                         expert_u[layer], expert_v[layer])
    return rms_norm(x, final_norm)
```

```python
import functools
import math

import jax
import jax.numpy as jnp
from jax import lax
from jax.experimental import pallas as pl
from jax.experimental.pallas import tpu as pltpu

F32 = jnp.float32
BF16 = jnp.bfloat16

RMS_EPS = 1e-6
SB_HEADS = 8
SB_HEAD_DIM = 64
SB_WIDTH = SB_HEADS * SB_HEAD_DIM
CONV_K = 3
PEER_HEADS = 8
N_KEYS = 128
PEER_TOPK = 16
LANES = 128
SUBLANES = 8
VMEM_LIMIT_BYTES = 56 * 1024 * 1024

IN_TM = 256
ATT_TQ = 256
ATT_TK = 256
MID_TM = 256
TOPK_TN = 128
PEER_TN = 512
PEER_TE = 1024


def _cparams(sem):
    return pltpu.CompilerParams(dimension_semantics=sem, vmem_limit_bytes=VMEM_LIMIT_BYTES)


def _inproj_kernel(x_ref, nm_ref, w_ref, cw_ref, cb_ref, qkv_ref, cbo_ref, gate_ref, carry_ref,
                   *, tiles_per_seq, width):
    i = pl.program_id(0)
    x = x_ref[...]
    ms = jnp.mean(x * x, axis=-1, keepdims=True)
    h = (x * lax.rsqrt(ms + RMS_EPS) * nm_ref[...]).astype(BF16)
    o3 = 3 * width
    qkv_ref[...] = jnp.dot(h, w_ref[:, 0:o3], preferred_element_type=F32).astype(BF16)

    c = jnp.dot(h, w_ref[:, o3:2 * o3], preferred_element_type=F32)
    cb, cc, cu = c[:, 0:width], c[:, width:2 * width], c[:, 2 * width:3 * width]
    z = cc * cu
    tm = z.shape[0]

    @pl.when(i % tiles_per_seq == 0)
    def _():
        carry_ref[...] = jnp.zeros_like(carry_ref)

    prev = carry_ref[...]
    rows = lax.broadcasted_iota(jnp.int32, z.shape, 0)
    z1 = jnp.where(rows == 0, prev[7:8, :], pltpu.roll(z, 1, 0))
    z2 = jnp.where(rows == 0, prev[6:7, :], jnp.where(rows == 1, prev[7:8, :], pltpu.roll(z, 2, 0)))
    y = cw_ref[0:1, :] * z2 + cw_ref[1:2, :] * z1 + cw_ref[2:3, :] * z + cb_ref[...]
    cbo_ref[...] = (cb * y).astype(BF16)
    carry_ref[...] = z[tm - SUBLANES:, :]

    g = jnp.dot(h, w_ref[:, 2 * o3:], preferred_element_type=F32)
    gate_ref[...] = jax.nn.sigmoid(g)


def _inproj(x2, norm_mix, w_in, conv_w, conv_b, seq):
    n, d = x2.shape
    width = SB_WIDTH
    in_width = w_in.shape[1]
    tm = IN_TM
    kern = functools.partial(_inproj_kernel, tiles_per_seq=seq // tm, width=width)
    return pl.pallas_call(
        kern,
        grid=(n // tm,),
        in_specs=[
            pl.BlockSpec((tm, d), lambda i: (i, 0)),
            pl.BlockSpec((1, d), lambda i: (0, 0)),
            pl.BlockSpec((d, in_width), lambda i: (0, 0)),
            pl.BlockSpec((CONV_K, width), lambda i: (0, 0)),
            pl.BlockSpec((1, width), lambda i: (0, 0)),
        ],
        out_specs=[
            pl.BlockSpec((tm, 3 * width), lambda i: (i, 0)),
            pl.BlockSpec((tm, width), lambda i: (i, 0)),
            pl.BlockSpec((tm, 2 * d), lambda i: (i, 0)),
        ],
        out_shape=[
            jax.ShapeDtypeStruct((n, 3 * width), BF16),
            jax.ShapeDtypeStruct((n, width), BF16),
            jax.ShapeDtypeStruct((n, 2 * d), F32),
        ],
        scratch_shapes=[pltpu.VMEM((SUBLANES, width), F32)],
        compiler_params=_cparams(("arbitrary",)),
        name="inproj",
    )(x2, norm_mix, w_in, conv_w, conv_b)


def _att_block(qh, kj, vj, tri, c, o, mask):
    z = lax.dot_general(qh, kj, (((1,), (1,)), ((), ())), preferred_element_type=F32)
    sp = jnp.maximum(z, 0.0) + jnp.log(1.0 + jnp.exp(-jnp.abs(z)))
    l1mb = -sp
    if mask is not None:
        l1mb = jnp.where(mask, l1mb, 0.0)
    hi = l1mb.astype(BF16)
    lo = (l1mb - hi.astype(F32)).astype(BF16)
    excl = (jnp.dot(hi, tri, preferred_element_type=F32)
            + jnp.dot(lo, tri, preferred_element_type=F32))
    a = jnp.exp((z - sp) + excl + c)
    if mask is not None:
        a = jnp.where(mask, a, 0.0)
    o = o + jnp.dot(a.astype(BF16), vj, preferred_element_type=F32)
    c = c + jnp.sum(l1mb, axis=-1, keepdims=True)
    return c, o


def _attention_kernel(q_ref, k_ref, v_ref, o_ref, *, tq, tk, scale):
    i = pl.program_id(2)
    q = q_ref[...] * jnp.asarray(scale, BF16)
    lane = lax.broadcasted_iota(jnp.int32, q.shape, 1)
    kr = lax.broadcasted_iota(jnp.int32, (tk, tk), 0)
    kc = lax.broadcasted_iota(jnp.int32, (tk, tk), 1)
    tri = jnp.where(kr > kc, 1.0, 0.0).astype(BF16)
    qr = lax.broadcasted_iota(jnp.int32, (tq, tk), 0)
    qc = lax.broadcasted_iota(jnp.int32, (tq, tk), 1)
    diag_mask = qc < qr

    outs = []
    for head in range(2):
        in_head = (lane < SB_HEAD_DIM) if head == 0 else (lane >= SB_HEAD_DIM)
        qh = jnp.where(in_head, q, jnp.zeros_like(q))
        c0 = jnp.zeros((tq, 1), F32)
        o0 = jnp.zeros((tq, LANES), F32)
        start = pl.multiple_of(i * tk, tk)
        c1, o1 = _att_block(qh, k_ref[pl.ds(start, tk), :], v_ref[pl.ds(start, tk), :], tri, c0, o0, diag_mask)

        def body(jj, carry, qh=qh):
            c, o = carry
            s = pl.multiple_of((i - jj) * tk, tk)
            return _att_block(qh, k_ref[pl.ds(s, tk), :], v_ref[pl.ds(s, tk), :], tri, c, o, None)

        _, o2 = lax.fori_loop(1, i + 1, body, (c1, o1))
        outs.append(o2)
    o_ref[...] = jnp.where(lane < SB_HEAD_DIM, outs[0], outs[1]).astype(o_ref.dtype)


def _attention(qkv, batch, seq):
    n = qkv.shape[0]
    tq, tk = ATT_TQ, ATT_TK
    assert tq == tk
    pairs = SB_WIDTH // LANES
    nq = seq // tq
    kern = functools.partial(_attention_kernel, tq=tq, tk=tk, scale=SB_HEAD_DIM ** -0.5)
    return pl.pallas_call(
        kern,
        grid=(batch, pairs, nq),
        in_specs=[
            pl.BlockSpec((tq, LANES), lambda b, p, i: (b * nq + i, p)),
            pl.BlockSpec((seq, LANES), lambda b, p, i: (b, pairs + p)),
            pl.BlockSpec((seq, LANES), lambda b, p, i: (b, 2 * pairs + p)),
        ],
        out_specs=pl.BlockSpec((tq, LANES), lambda b, p, i: (b * nq + i, p)),
        out_shape=jax.ShapeDtypeStruct((n, SB_WIDTH), BF16),
        compiler_params=_cparams(("arbitrary", "arbitrary", "arbitrary")),
        name="stickbreak_attention",
    )(qkv, qkv, qkv)


def _mid_kernel(att_ref, cbo_ref, gate_ref, x_ref, wa_ref, wb_ref, wo_ref, nf_ref, wqt_ref, keys_ref,
                x1_ref, xnt_ref, st_ref):
    d = x_ref.shape[1]
    ya = jnp.dot(att_ref[...], wa_ref[...], preferred_element_type=F32)
    yb = jnp.dot(cbo_ref[...], wb_ref[...], preferred_element_type=F32)
    merged = (gate_ref[:, 0:d] * ya + gate_ref[:, d:2 * d] * yb).astype(BF16)
    x1 = x_ref[...] + jnp.dot(merged, wo_ref[...], preferred_element_type=F32)
    x1_ref[...] = x1
    ms = jnp.mean(x1 * x1, axis=-1, keepdims=True)
    xn = x1 * lax.rsqrt(ms + RMS_EPS) * nf_ref[...]
    xnt_ref[...] = xn.T.astype(BF16)
    qt = lax.dot_general(wqt_ref[...], xn.astype(BF16), (((1,), (1,)), ((), ())),
                         preferred_element_type=F32)
    for hp in range(st_ref.shape[0]):
        qhp = qt[hp * N_KEYS:(hp + 1) * N_KEYS, :].astype(BF16)
        st_ref[hp] = jnp.dot(keys_ref[hp], qhp, preferred_element_type=F32)


def _mid(att, cbo, gates, x2, wa, wb, wo, norm_ffn, wqt, keys):
    n, d = x2.shape
    tm = MID_TM
    nhp = keys.shape[0]
    const2 = lambda i: (0, 0)
    return pl.pallas_call(
        _mid_kernel,
        grid=(n // tm,),
        in_specs=[
            pl.BlockSpec((tm, SB_WIDTH), lambda i: (i, 0)),
            pl.BlockSpec((tm, SB_WIDTH), lambda i: (i, 0)),
            pl.BlockSpec((tm, 2 * d), lambda i: (i, 0)),
            pl.BlockSpec((tm, d), lambda i: (i, 0)),
            pl.BlockSpec(wa.shape, const2),
            pl.BlockSpec(wb.shape, const2),
            pl.BlockSpec(wo.shape, const2),
            pl.BlockSpec((1, d), const2),
            pl.BlockSpec(wqt.shape, const2),
            pl.BlockSpec(keys.shape, lambda i: (0, 0, 0)),
        ],
        out_specs=[
            pl.BlockSpec((tm, d), lambda i: (i, 0)),
            pl.BlockSpec((d, tm), lambda i: (0, i)),
            pl.BlockSpec((nhp, N_KEYS, tm), lambda i: (0, 0, i)),
        ],
        out_shape=[
            jax.ShapeDtypeStruct((n, d), F32),
            jax.ShapeDtypeStruct((d, n), BF16),
            jax.ShapeDtypeStruct((nhp, N_KEYS, n), F32),
        ],
        compiler_params=_cparams(("arbitrary",)),
        name="mid_proj",
    )(att, cbo, gates, x2, wa, wb, wo, norm_ffn, wqt, keys)


def _candidate_cells():
    return [(a, b) for a in range(PEER_TOPK) for b in range(PEER_TOPK) if (a + 1) * (b + 1) <= PEER_TOPK]


def _topk_kernel(st_ref, g1_ref, n1_ref, g2_ref, r2_ref, rank_scr, tops_scr):
    k = PEER_TOPK
    neg_inf = jnp.asarray(-jnp.inf, F32)
    kio = lax.broadcasted_iota(jnp.int32, (N_KEYS, LANES), 0).astype(F32)
    sub = lax.broadcasted_iota(jnp.int32, (SUBLANES, LANES), 0)

    def hp_body(hp, _):
        x = st_ref[hp]
        rank = jnp.full((N_KEYS, LANES), float(k), F32)
        h = hp // 2
        p = hp % 2
        for r in range(k):
            m = jnp.max(x, axis=0, keepdims=True)
            first = jnp.min(jnp.where(x == m, kio, float(N_KEYS)), axis=0, keepdims=True)
            hit = kio == first
            rank = jnp.where(hit, float(r), rank)
            x = jnp.where(hit, neg_inf, x)
            slot = p * k + r
            tops_scr[slot] = jnp.where(sub == h, m, tops_scr[slot])
        rank_scr[hp] = rank
        return 0

    lax.fori_loop(0, st_ref.shape[0], hp_body, 0)

    t1 = [tops_scr[a] for a in range(k)]
    t2 = [tops_scr[k + b] for b in range(k)]
    cells = _candidate_cells()
    val = {ab: t1[ab[0]] + t2[ab[1]] for ab in cells}
    beat = {ab: jnp.full((SUBLANES, LANES), float((ab[0] + 1) * (ab[1] + 1) - 1), F32) for ab in cells}
    for ip, p_ in enumerate(cells):
        for q_ in cells[ip + 1:]:
            if p_[0] <= q_[0] and p_[1] <= q_[1]:
                continue
            t = jnp.where(val[p_] >= val[q_], 1.0, 0.0)
            beat[q_] = beat[q_] + t
            beat[p_] = beat[p_] + (1.0 - t)
    top = val[(0, 0)]
    zsum = jnp.zeros((SUBLANES, LANES), F32)
    n_sel = [jnp.zeros((SUBLANES, LANES), F32) for _ in range(k)]
    for ab in cells:
        sel = beat[ab] < float(k)
        zsum = zsum + jnp.where(sel, jnp.exp(val[ab] - top), 0.0)
        n_sel[ab[0]] = n_sel[ab[0]] + jnp.where(sel, 1.0, 0.0)
    zinv = 1.0 / zsum

    for h in range(PEER_HEADS):
        rank1 = rank_scr[2 * h]
        rank2 = rank_scr[2 * h + 1]
        m1 = t1[0][h:h + 1, :]
        m2 = t2[0][h:h + 1, :]
        g1_ref[h] = jnp.where(rank1 < float(k), jnp.exp(st_ref[2 * h] - m1), 0.0)
        g2 = jnp.where(rank2 < float(k), jnp.exp(st_ref[2 * h + 1] - m2) * zinv[h:h + 1, :], 0.0)
        g2_ref[h] = g2.astype(BF16)
        r2_ref[h] = rank2.astype(BF16)
        n1 = jnp.zeros((N_KEYS, LANES), F32)
        for a in range(k):
            n1 = jnp.where(rank1 == float(a), n_sel[a][h:h + 1, :], n1)
        n1_ref[h] = n1


def _topk(st):
    nhp, nk, n = st.shape
    tn = TOPK_TN
    heads = nhp // 2
    spec = pl.BlockSpec((heads, nk, tn), lambda i: (0, 0, i))
    return pl.pallas_call(
        _topk_kernel,
        grid=(n // tn,),
        in_specs=[pl.BlockSpec((nhp, nk, tn), lambda i: (0, 0, i))],
        out_specs=[spec, spec, spec, spec],
        out_shape=[
            jax.ShapeDtypeStruct((heads, nk, n), F32),
            jax.ShapeDtypeStruct((heads, nk, n), F32),
            jax.ShapeDtypeStruct((heads, nk, n), BF16),
            jax.ShapeDtypeStruct((heads, nk, n), BF16),
        ],
        scratch_shapes=[
            pltpu.VMEM((nhp, nk, tn), F32),
            pltpu.VMEM((2 * PEER_TOPK, SUBLANES, tn), F32),
        ],
        compiler_params=_cparams(("arbitrary",)),
        name="peer_topk",
    )(st)


def _peer_kernel(xnt_ref, u_ref, vt_ref, g1_ref, n1_ref, g2_ref, r2_ref, x1_ref, fn_ref, out_ref,
                 s_scr, p_scr, acc_scr):
    e = pl.program_id(1)
    te = u_ref.shape[0]

    @pl.when(e == 0)
    def _():
        acc_scr[...] = jnp.zeros_like(acc_scr)

    s_scr[...] = jnp.dot(u_ref[...], xnt_ref[...], preferred_element_type=F32)

    def i1_body(j, _):
        rows = pl.ds(pl.multiple_of(j * N_KEYS, N_KEYS), N_KEYS)
        s = s_scr[rows, :]
        act = 0.5 * s * (1.0 + lax.erf(s * math.sqrt(0.5)))
        w = jnp.zeros(s.shape, BF16)
        for h in range(PEER_HEADS):
            g1 = g1_ref[h, pl.ds(j, 1), :].astype(BF16)
            n1 = n1_ref[h, pl.ds(j, 1), :].astype(BF16)
            w = w + jnp.where(r2_ref[h] < n1, g2_ref[h] * g1, jnp.zeros_like(w))
        p_scr[rows, :] = act.astype(BF16) * w
        return 0

    lax.fori_loop(0, te // N_KEYS, i1_body, 0)
    acc_scr[...] += jnp.dot(vt_ref[...], p_scr[...], preferred_element_type=F32)

    @pl.when(e == pl.num_programs(1) - 1)
    def _():
        y = x1_ref[...] + acc_scr[...].T
        ms = jnp.mean(y * y, axis=-1, keepdims=True)
        out_ref[...] = y * lax.rsqrt(ms + RMS_EPS) * fn_ref[...]


def _peer(xnt, u, vt, g1, n1, g2, r2, x1, final_norm):
    d, n = xnt.shape
    n_exp = u.shape[0]
    tn, te = PEER_TN, PEER_TE
    heads = g1.shape[0]
    i1_per_tile = te // N_KEYS
    assert i1_per_tile == SUBLANES
    return pl.pallas_call(
        _peer_kernel,
        grid=(n // tn, n_exp // te),
        in_specs=[
            pl.BlockSpec((d, tn), lambda t, e: (0, t)),
            pl.BlockSpec((te, d), lambda t, e: (e, 0)),
            pl.BlockSpec((d, te), lambda t, e: (0, e)),
            pl.BlockSpec((heads, i1_per_tile, tn), lambda t, e: (0, e, t)),
            pl.BlockSpec((heads, i1_per_tile, tn), lambda t, e: (0, e, t)),
            pl.BlockSpec((heads, N_KEYS, tn), lambda t, e: (0, 0, t)),
            pl.BlockSpec((heads, N_KEYS, tn), lambda t, e: (0, 0, t)),
            pl.BlockSpec((tn, d), lambda t, e: (t, 0)),
            pl.BlockSpec((1, d), lambda t, e: (0, 0)),
        ],
        out_specs=pl.BlockSpec((tn, d), lambda t, e: (t, 0)),
        out_shape=jax.ShapeDtypeStruct((n, d), F32),
        scratch_shapes=[
            pltpu.VMEM((te, tn), F32),
            pltpu.VMEM((te, tn), BF16),
            pltpu.VMEM((d, tn), F32),
        ],
        compiler_params=_cparams(("arbitrary", "arbitrary")),
        name="peer_dense",
    )(xnt, u, vt, g1, n1, g2, r2, x1, final_norm)


def _layer(x2, batch, seq, norm_mix, w_in, conv_w, conv_b, w_a, w_b, w_o, norm_ffn, w_q, sub_keys,
           expert_u, expert_v, out_norm):
    d = x2.shape[1]
    qkv, cbo, gates = _inproj(x2, norm_mix.reshape(1, d), w_in.astype(BF16),
                              conv_w.reshape(CONV_K, -1), conv_b.reshape(1, -1), seq)
    att = _attention(qkv, batch, seq)
    keys = sub_keys.reshape(-1, N_KEYS, sub_keys.shape[-1]).astype(BF16)
    x1, xnt, st = _mid(att, cbo, gates, x2, w_a.astype(BF16), w_b.astype(BF16), w_o.astype(BF16),
                       norm_ffn.reshape(1, d), w_q.T.astype(BF16), keys)
    g1, n1, g2, r2 = _topk(st)
    return _peer(xnt, expert_u.astype(BF16), expert_v.T.astype(BF16), g1, n1, g2, r2, x1, out_norm)


def kernel(x, norm_mix, w_in, conv_w, conv_b, w_branch_a, w_branch_b, w_out, norm_ffn, w_q, sub_keys,
           expert_u, expert_v, final_norm):
    batch, seq, d = x.shape
    depth = w_in.shape[0]
    assert depth == 1, "the final RMSNorm is fused into the last layer's PEER kernel"
    x2 = x.reshape(batch * seq, d)
    out = _layer(x2, batch, seq, norm_mix[0], w_in[0], conv_w[0], conv_b[0], w_branch_a[0],
                 w_branch_b[0], w_out[0], norm_ffn[0], w_q[0], sub_keys[0], expert_u[0], expert_v[0],
                 final_norm.reshape(1, d))
    return out.reshape(batch, seq, d)
```

```python
import functools
import math

import jax
import jax.numpy as jnp
from jax import lax
from jax.experimental import pallas as pl
from jax.experimental.pallas import tpu as pltpu

F32 = jnp.float32
BF16 = jnp.bfloat16

RMS_EPS = 1e-6
SB_HEADS = 8
SB_HEAD_DIM = 64
SB_WIDTH = SB_HEADS * SB_HEAD_DIM
CONV_K = 3
PEER_HEADS = 8
N_KEYS = 128
PEER_TOPK = 16
LANES = 128
SUBLANES = 8
VMEM_LIMIT_BYTES = 56 * 1024 * 1024
EXP_UNDERFLOW_LOG = -110.0

IN_TM = 256
ATT_TQ = 256
ATT_TK = 256
MID_TM = 256
TOPK_TN = 128
PEER_TN = 512
PEER_TE = 1024


def _cparams(sem):
    return pltpu.CompilerParams(dimension_semantics=sem, vmem_limit_bytes=VMEM_LIMIT_BYTES)


def _inproj_kernel(x_ref, nm_ref, w_ref, cw_ref, cb_ref, qkv_ref, cbo_ref, gate_ref, carry_ref,
                   *, tiles_per_seq, width):
    i = pl.program_id(0)
    x = x_ref[...]
    ms = jnp.mean(x * x, axis=-1, keepdims=True)
    h = (x * lax.rsqrt(ms + RMS_EPS) * nm_ref[...]).astype(BF16)
    o3 = 3 * width
    qkv_ref[...] = jnp.dot(h, w_ref[:, 0:o3], preferred_element_type=F32).astype(BF16)

    c = jnp.dot(h, w_ref[:, o3:2 * o3], preferred_element_type=F32)
    cb, cc, cu = c[:, 0:width], c[:, width:2 * width], c[:, 2 * width:3 * width]
    z = cc * cu
    tm = z.shape[0]

    @pl.when(i % tiles_per_seq == 0)
    def _():
        carry_ref[...] = jnp.zeros_like(carry_ref)

    prev = carry_ref[...]
    rows = lax.broadcasted_iota(jnp.int32, z.shape, 0)
    z1 = jnp.where(rows == 0, prev[7:8, :], pltpu.roll(z, 1, 0))
    z2 = jnp.where(rows == 0, prev[6:7, :], jnp.where(rows == 1, prev[7:8, :], pltpu.roll(z, 2, 0)))
    y = cw_ref[0:1, :] * z2 + cw_ref[1:2, :] * z1 + cw_ref[2:3, :] * z + cb_ref[...]
    cbo_ref[...] = (cb * y).astype(BF16)
    carry_ref[...] = z[tm - SUBLANES:, :]

    g = jnp.dot(h, w_ref[:, 2 * o3:], preferred_element_type=F32)
    gate_ref[...] = jax.nn.sigmoid(g)


def _inproj(x2, norm_mix, w_in, conv_w, conv_b, seq):
    n, d = x2.shape
    width = SB_WIDTH
    in_width = w_in.shape[1]
    tm = IN_TM
    kern = functools.partial(_inproj_kernel, tiles_per_seq=seq // tm, width=width)
    return pl.pallas_call(
        kern,
        grid=(n // tm,),
        in_specs=[
            pl.BlockSpec((tm, d), lambda i: (i, 0)),
            pl.BlockSpec((1, d), lambda i: (0, 0)),
            pl.BlockSpec((d, in_width), lambda i: (0, 0)),
            pl.BlockSpec((CONV_K, width), lambda i: (0, 0)),
            pl.BlockSpec((1, width), lambda i: (0, 0)),
        ],
        out_specs=[
            pl.BlockSpec((tm, 3 * width), lambda i: (i, 0)),
            pl.BlockSpec((tm, width), lambda i: (i, 0)),
            pl.BlockSpec((tm, 2 * d), lambda i: (i, 0)),
        ],
        out_shape=[
            jax.ShapeDtypeStruct((n, 3 * width), BF16),
            jax.ShapeDtypeStruct((n, width), BF16),
            jax.ShapeDtypeStruct((n, 2 * d), F32),
        ],
        scratch_shapes=[pltpu.VMEM((SUBLANES, width), F32)],
        compiler_params=_cparams(("arbitrary",)),
        name="inproj",
    )(x2, norm_mix, w_in, conv_w, conv_b)


def _att_block(qh, kj, vj, tri, c, o, mask):
    z = lax.dot_general(qh, kj, (((1,), (1,)), ((), ())), preferred_element_type=F32)
    sp = jnp.maximum(z, 0.0) + jnp.log(1.0 + jnp.exp(-jnp.abs(z)))
    l1mb = -sp
    if mask is not None:
        l1mb = jnp.where(mask, l1mb, 0.0)
    hi = l1mb.astype(BF16)
    lo = (l1mb - hi.astype(F32)).astype(BF16)
    excl = (jnp.dot(hi, tri, preferred_element_type=F32)
            + jnp.dot(lo, tri, preferred_element_type=F32))
    a = jnp.exp((z - sp) + excl + c)
    if mask is not None:
        a = jnp.where(mask, a, 0.0)
    o = o + jnp.dot(a.astype(BF16), vj, preferred_element_type=F32)
    c = c + jnp.sum(l1mb, axis=-1, keepdims=True)
    return c, o


def _attention_kernel(q_ref, k_ref, v_ref, o_ref, *, tq, tk, scale):
    i = pl.program_id(2)
    q = q_ref[...] * jnp.asarray(scale, BF16)
    lane = lax.broadcasted_iota(jnp.int32, q.shape, 1)
    kr = lax.broadcasted_iota(jnp.int32, (tk, tk), 0)
    kc = lax.broadcasted_iota(jnp.int32, (tk, tk), 1)
    tri = jnp.where(kr > kc, 1.0, 0.0).astype(BF16)
    qr = lax.broadcasted_iota(jnp.int32, (tq, tk), 0)
    qc = lax.broadcasted_iota(jnp.int32, (tq, tk), 1)
    diag_mask = qc < qr

    zero = jnp.zeros_like(q)
    qs = (jnp.where(lane < SB_HEAD_DIM, q, zero), jnp.where(lane >= SB_HEAD_DIM, q, zero))

    def both_heads(j, cs, os_, mask):
        s = pl.multiple_of(j * tk, tk)
        kj = k_ref[pl.ds(s, tk), :]
        vj = v_ref[pl.ds(s, tk), :]
        new = [_att_block(qs[h], kj, vj, tri, cs[h], os_[h], mask) for h in range(2)]
        return (new[0][0], new[1][0]), (new[0][1], new[1][1])

    def alive(cs):
        return (jnp.max(jnp.maximum(cs[0], cs[1])) >= EXP_UNDERFLOW_LOG).astype(jnp.int32)

    c0 = jnp.zeros((tq, 1), F32)
    o0 = jnp.zeros((tq, LANES), F32)
    cs, os_ = both_heads(i, (c0, c0), (o0, o0), diag_mask)

    def cond(state):
        jj, _, _, live = state
        return jnp.logical_and(jj <= i, live > 0)

    def body(state):
        jj, cs, os_, _ = state
        cs, os_ = both_heads(i - jj, cs, os_, None)
        return jj + 1, cs, os_, alive(cs)

    _, _, os_, _ = lax.while_loop(cond, body, (jnp.int32(1), cs, os_, alive(cs)))
    o_ref[...] = jnp.where(lane < SB_HEAD_DIM, os_[0], os_[1]).astype(o_ref.dtype)


def _attention(qkv, batch, seq):
    n = qkv.shape[0]
    tq, tk = ATT_TQ, ATT_TK
    assert tq == tk
    pairs = SB_WIDTH // LANES
    nq = seq // tq
    kern = functools.partial(_attention_kernel, tq=tq, tk=tk, scale=SB_HEAD_DIM ** -0.5)
    return pl.pallas_call(
        kern,
        grid=(batch, pairs, nq),
        in_specs=[
            pl.BlockSpec((tq, LANES), lambda b, p, i: (b * nq + i, p)),
            pl.BlockSpec((seq, LANES), lambda b, p, i: (b, pairs + p)),
            pl.BlockSpec((seq, LANES), lambda b, p, i: (b, 2 * pairs + p)),
        ],
        out_specs=pl.BlockSpec((tq, LANES), lambda b, p, i: (b * nq + i, p)),
        out_shape=jax.ShapeDtypeStruct((n, SB_WIDTH), BF16),
        compiler_params=_cparams(("arbitrary", "arbitrary", "arbitrary")),
        name="stickbreak_attention",
    )(qkv, qkv, qkv)


def _mid_kernel(att_ref, cbo_ref, gate_ref, x_ref, wa_ref, wb_ref, wo_ref, nf_ref, wqt_ref, keys_ref,
                x1_ref, xnt_ref, st_ref):
    d = x_ref.shape[1]
    ya = jnp.dot(att_ref[...], wa_ref[...], preferred_element_type=F32)
    yb = jnp.dot(cbo_ref[...], wb_ref[...], preferred_element_type=F32)
    merged = (gate_ref[:, 0:d] * ya + gate_ref[:, d:2 * d] * yb).astype(BF16)
    x1 = x_ref[...] + jnp.dot(merged, wo_ref[...], preferred_element_type=F32)
    x1_ref[...] = x1
    ms = jnp.mean(x1 * x1, axis=-1, keepdims=True)
    xn = x1 * lax.rsqrt(ms + RMS_EPS) * nf_ref[...]
    xnt_ref[...] = xn.T.astype(BF16)
    qt = lax.dot_general(wqt_ref[...], xn.astype(BF16), (((1,), (1,)), ((), ())),
                         preferred_element_type=F32)
    for hp in range(st_ref.shape[0]):
        qhp = qt[hp * N_KEYS:(hp + 1) * N_KEYS, :].astype(BF16)
        st_ref[hp] = jnp.dot(keys_ref[hp], qhp, preferred_element_type=F32)


def _mid(att, cbo, gates, x2, wa, wb, wo, norm_ffn, wqt, keys):
    n, d = x2.shape
    tm = MID_TM
    nhp = keys.shape[0]
    const2 = lambda i: (0, 0)
    return pl.pallas_call(
        _mid_kernel,
        grid=(n // tm,),
        in_specs=[
            pl.BlockSpec((tm, SB_WIDTH), lambda i: (i, 0)),
            pl.BlockSpec((tm, SB_WIDTH), lambda i: (i, 0)),
            pl.BlockSpec((tm, 2 * d), lambda i: (i, 0)),
            pl.BlockSpec((tm, d), lambda i: (i, 0)),
            pl.BlockSpec(wa.shape, const2),
            pl.BlockSpec(wb.shape, const2),
            pl.BlockSpec(wo.shape, const2),
            pl.BlockSpec((1, d), const2),
            pl.BlockSpec(wqt.shape, const2),
            pl.BlockSpec(keys.shape, lambda i: (0, 0, 0)),
        ],
        out_specs=[
            pl.BlockSpec((tm, d), lambda i: (i, 0)),
            pl.BlockSpec((d, tm), lambda i: (0, i)),
            pl.BlockSpec((nhp, N_KEYS, tm), lambda i: (0, 0, i)),
        ],
        out_shape=[
            jax.ShapeDtypeStruct((n, d), F32),
            jax.ShapeDtypeStruct((d, n), BF16),
            jax.ShapeDtypeStruct((nhp, N_KEYS, n), F32),
        ],
        compiler_params=_cparams(("arbitrary",)),
        name="mid_proj",
    )(att, cbo, gates, x2, wa, wb, wo, norm_ffn, wqt, keys)


def _candidate_cells():
    return [(a, b) for a in range(PEER_TOPK) for b in range(PEER_TOPK) if (a + 1) * (b + 1) <= PEER_TOPK]


def _topk_kernel(st_ref, g1_ref, n1_ref, g2_ref, r2_ref, rank_scr, tops_scr):
    k = PEER_TOPK
    neg_inf = jnp.asarray(-jnp.inf, F32)
    kio = lax.broadcasted_iota(jnp.int32, (N_KEYS, LANES), 0).astype(F32)
    sub = lax.broadcasted_iota(jnp.int32, (SUBLANES, LANES), 0)

    def hp_body(hp, _):
        x = st_ref[hp]
        rank = jnp.full((N_KEYS, LANES), float(k), F32)
        h = hp // 2
        p = hp % 2
        for r in range(k):
            m = jnp.max(x, axis=0, keepdims=True)
            first = jnp.min(jnp.where(x == m, kio, float(N_KEYS)), axis=0, keepdims=True)
            hit = kio == first
            rank = jnp.where(hit, float(r), rank)
            x = jnp.where(hit, neg_inf, x)
            slot = p * k + r
            tops_scr[slot] = jnp.where(sub == h, m, tops_scr[slot])
        rank_scr[hp] = rank
        return 0

    lax.fori_loop(0, st_ref.shape[0], hp_body, 0)

    t1 = [tops_scr[a] for a in range(k)]
    t2 = [tops_scr[k + b] for b in range(k)]
    cells = _candidate_cells()
    val = {ab: t1[ab[0]] + t2[ab[1]] for ab in cells}
    beat = {ab: jnp.full((SUBLANES, LANES), float((ab[0] + 1) * (ab[1] + 1) - 1), F32) for ab in cells}
    for ip, p_ in enumerate(cells):
        for q_ in cells[ip + 1:]:
            if p_[0] <= q_[0] and p_[1] <= q_[1]:
                continue
            t = jnp.where(val[p_] >= val[q_], 1.0, 0.0)
            beat[q_] = beat[q_] + t
            beat[p_] = beat[p_] + (1.0 - t)
    top = val[(0, 0)]
    zsum = jnp.zeros((SUBLANES, LANES), F32)
    n_sel = [jnp.zeros((SUBLANES, LANES), F32) for _ in range(k)]
    for ab in cells:
        sel = beat[ab] < float(k)
        zsum = zsum + jnp.where(sel, jnp.exp(val[ab] - top), 0.0)
        n_sel[ab[0]] = n_sel[ab[0]] + jnp.where(sel, 1.0, 0.0)
    zinv = 1.0 / zsum

    for h in range(PEER_HEADS):
        rank1 = rank_scr[2 * h]
        rank2 = rank_scr[2 * h + 1]
        m1 = t1[0][h:h + 1, :]
        m2 = t2[0][h:h + 1, :]
        g1_ref[h] = jnp.where(rank1 < float(k), jnp.exp(st_ref[2 * h] - m1), 0.0)
        g2 = jnp.where(rank2 < float(k), jnp.exp(st_ref[2 * h + 1] - m2) * zinv[h:h + 1, :], 0.0)
        g2_ref[h] = g2.astype(BF16)
        r2_ref[h] = rank2.astype(BF16)
        n1 = jnp.zeros((N_KEYS, LANES), F32)
        for a in range(k):
            n1 = jnp.where(rank1 == float(a), n_sel[a][h:h + 1, :], n1)
        n1_ref[h] = n1


def _topk(st):
    nhp, nk, n = st.shape
    tn = TOPK_TN
    heads = nhp // 2
    spec = pl.BlockSpec((heads, nk, tn), lambda i: (0, 0, i))
    return pl.pallas_call(
        _topk_kernel,
        grid=(n // tn,),
        in_specs=[pl.BlockSpec((nhp, nk, tn), lambda i: (0, 0, i))],
        out_specs=[spec, spec, spec, spec],
        out_shape=[
            jax.ShapeDtypeStruct((heads, nk, n), F32),
            jax.ShapeDtypeStruct((heads, nk, n), F32),
            jax.ShapeDtypeStruct((heads, nk, n), BF16),
            jax.ShapeDtypeStruct((heads, nk, n), BF16),
        ],
        scratch_shapes=[
            pltpu.VMEM((nhp, nk, tn), F32),
            pltpu.VMEM((2 * PEER_TOPK, SUBLANES, tn), F32),
        ],
        compiler_params=_cparams(("arbitrary",)),
        name="peer_topk",
    )(st)


def _peer_kernel(xnt_ref, u_ref, vt_ref, g1_ref, n1_ref, g2_ref, r2_ref, x1_ref, fn_ref, out_ref,
                 s_scr, p_scr, acc_scr):
    e = pl.program_id(1)
    te = u_ref.shape[0]

    @pl.when(e == 0)
    def _():
        acc_scr[...] = jnp.zeros_like(acc_scr)

    s_scr[...] = jnp.dot(u_ref[...], xnt_ref[...], preferred_element_type=F32)

    def i1_body(j, _):
        rows = pl.ds(pl.multiple_of(j * N_KEYS, N_KEYS), N_KEYS)
        s = s_scr[rows, :]
        act = 0.5 * s * (1.0 + lax.erf(s * math.sqrt(0.5)))
        w = jnp.zeros(s.shape, BF16)
        for h in range(PEER_HEADS):
            g1 = g1_ref[h, pl.ds(j, 1), :].astype(BF16)
            n1 = n1_ref[h, pl.ds(j, 1), :].astype(BF16)
            w = w + jnp.where(r2_ref[h] < n1, g2_ref[h] * g1, jnp.zeros_like(w))
        p_scr[rows, :] = act.astype(BF16) * w
        return 0

    lax.fori_loop(0, te // N_KEYS, i1_body, 0)
    acc_scr[...] += jnp.dot(vt_ref[...], p_scr[...], preferred_element_type=F32)

    @pl.when(e == pl.num_programs(1) - 1)
    def _():
        y = x1_ref[...] + acc_scr[...].T
        ms = jnp.mean(y * y, axis=-1, keepdims=True)
        out_ref[...] = y * lax.rsqrt(ms + RMS_EPS) * fn_ref[...]


def _peer(xnt, u, vt, g1, n1, g2, r2, x1, final_norm):
    d, n = xnt.shape
    n_exp = u.shape[0]
    tn, te = PEER_TN, PEER_TE
    heads = g1.shape[0]
    i1_per_tile = te // N_KEYS
    assert i1_per_tile == SUBLANES
    return pl.pallas_call(
        _peer_kernel,
        grid=(n // tn, n_exp // te),
        in_specs=[
            pl.BlockSpec((d, tn), lambda t, e: (0, t)),
            pl.BlockSpec((te, d), lambda t, e: (e, 0)),
            pl.BlockSpec((d, te), lambda t, e: (0, e)),
            pl.BlockSpec((heads, i1_per_tile, tn), lambda t, e: (0, e, t)),
            pl.BlockSpec((heads, i1_per_tile, tn), lambda t, e: (0, e, t)),
            pl.BlockSpec((heads, N_KEYS, tn), lambda t, e: (0, 0, t)),
            pl.BlockSpec((heads, N_KEYS, tn), lambda t, e: (0, 0, t)),
            pl.BlockSpec((tn, d), lambda t, e: (t, 0)),
            pl.BlockSpec((1, d), lambda t, e: (0, 0)),
        ],
        out_specs=pl.BlockSpec((tn, d), lambda t, e: (t, 0)),
        out_shape=jax.ShapeDtypeStruct((n, d), F32),
        scratch_shapes=[
            pltpu.VMEM((te, tn), F32),
            pltpu.VMEM((te, tn), BF16),
            pltpu.VMEM((d, tn), F32),
        ],
        compiler_params=_cparams(("arbitrary", "arbitrary")),
        name="peer_dense",
    )(xnt, u, vt, g1, n1, g2, r2, x1, final_norm)


def _layer(x2, batch, seq, norm_mix, w_in, conv_w, conv_b, w_a, w_b, w_o, norm_ffn, w_q, sub_keys,
           expert_u, expert_v, out_norm):
    d = x2.shape[1]
    qkv, cbo, gates = _inproj(x2, norm_mix.reshape(1, d), w_in.astype(BF16),
                              conv_w.reshape(CONV_K, -1), conv_b.reshape(1, -1), seq)
    att = _attention(qkv, batch, seq)
    keys = sub_keys.reshape(-1, N_KEYS, sub_keys.shape[-1]).astype(BF16)
    x1, xnt, st = _mid(att, cbo, gates, x2, w_a.astype(BF16), w_b.astype(BF16), w_o.astype(BF16),
                       norm_ffn.reshape(1, d), w_q.T.astype(BF16), keys)
    g1, n1, g2, r2 = _topk(st)
    return _peer(xnt, expert_u.astype(BF16), expert_v.T.astype(BF16), g1, n1, g2, r2, x1, out_norm)


def kernel(x, norm_mix, w_in, conv_w, conv_b, w_branch_a, w_branch_b, w_out, norm_ffn, w_q, sub_keys,
           expert_u, expert_v, final_norm):
    batch, seq, d = x.shape
    depth = w_in.shape[0]
    assert depth == 1, "the final RMSNorm is fused into the last layer's PEER kernel"
    x2 = x.reshape(batch * seq, d)
    out = _layer(x2, batch, seq, norm_mix[0], w_in[0], conv_w[0], conv_b[0], w_branch_a[0],
                 w_branch_b[0], w_out[0], norm_ffn[0], w_q[0], sub_keys[0], expert_u[0], expert_v[0],
                 final_norm.reshape(1, d))
    return out.reshape(batch, seq, d)
```

```python
import functools
import math

import jax
import jax.numpy as jnp
from jax import lax
from jax.experimental import pallas as pl
from jax.experimental.pallas import tpu as pltpu

F32 = jnp.float32
BF16 = jnp.bfloat16

RMS_EPS = 1e-6
SB_HEADS = 8
SB_HEAD_DIM = 64
SB_WIDTH = SB_HEADS * SB_HEAD_DIM
CONV_K = 3
PEER_HEADS = 8
N_KEYS = 128
PEER_TOPK = 16
LANES = 128
SUBLANES = 8
VMEM_LIMIT_BYTES = 56 * 1024 * 1024
EXP_UNDERFLOW_LOG = -110.0

IN_TM = 256
ATT_TQ = 256
ATT_TK = 256
MID_TM = 256
TOPK_TN = 128
PEER_TN = 512
PEER_TE = 1024


def _cparams(sem):
    return pltpu.CompilerParams(dimension_semantics=sem, vmem_limit_bytes=VMEM_LIMIT_BYTES)


def _inproj_kernel(x_ref, nm_ref, w_ref, cw_ref, cb_ref, qkv_ref, cbo_ref, gate_ref, carry_ref,
                   *, tiles_per_seq, width):
    i = pl.program_id(0)
    x = x_ref[...]
    ms = jnp.mean(x * x, axis=-1, keepdims=True)
    h = (x * lax.rsqrt(ms + RMS_EPS) * nm_ref[...]).astype(BF16)
    o3 = 3 * width
    qkv_ref[...] = jnp.dot(h, w_ref[:, 0:o3], preferred_element_type=F32).astype(BF16)

    c = jnp.dot(h, w_ref[:, o3:2 * o3], preferred_element_type=F32)
    cb, cc, cu = c[:, 0:width], c[:, width:2 * width], c[:, 2 * width:3 * width]
    z = cc * cu
    tm = z.shape[0]

    @pl.when(i % tiles_per_seq == 0)
    def _():
        carry_ref[...] = jnp.zeros_like(carry_ref)

    prev = carry_ref[...]
    rows = lax.broadcasted_iota(jnp.int32, z.shape, 0)
    z1 = jnp.where(rows == 0, prev[7:8, :], pltpu.roll(z, 1, 0))
    z2 = jnp.where(rows == 0, prev[6:7, :], jnp.where(rows == 1, prev[7:8, :], pltpu.roll(z, 2, 0)))
    y = cw_ref[0:1, :] * z2 + cw_ref[1:2, :] * z1 + cw_ref[2:3, :] * z + cb_ref[...]
    cbo_ref[...] = (cb * y).astype(BF16)
    carry_ref[...] = z[tm - SUBLANES:, :]

    g = jnp.dot(h, w_ref[:, 2 * o3:], preferred_element_type=F32)
    gate_ref[...] = jax.nn.sigmoid(g)


def _inproj(x2, norm_mix, w_in, conv_w, conv_b, seq):
    n, d = x2.shape
    width = SB_WIDTH
    in_width = w_in.shape[1]
    tm = IN_TM
    kern = functools.partial(_inproj_kernel, tiles_per_seq=seq // tm, width=width)
    return pl.pallas_call(
        kern,
        grid=(n // tm,),
        in_specs=[
            pl.BlockSpec((tm, d), lambda i: (i, 0)),
            pl.BlockSpec((1, d), lambda i: (0, 0)),
            pl.BlockSpec((d, in_width), lambda i: (0, 0)),
            pl.BlockSpec((CONV_K, width), lambda i: (0, 0)),
            pl.BlockSpec((1, width), lambda i: (0, 0)),
        ],
        out_specs=[
            pl.BlockSpec((tm, 3 * width), lambda i: (i, 0)),
            pl.BlockSpec((tm, width), lambda i: (i, 0)),
            pl.BlockSpec((tm, 2 * d), lambda i: (i, 0)),
        ],
        out_shape=[
            jax.ShapeDtypeStruct((n, 3 * width), BF16),
            jax.ShapeDtypeStruct((n, width), BF16),
            jax.ShapeDtypeStruct((n, 2 * d), F32),
        ],
        scratch_shapes=[pltpu.VMEM((SUBLANES, width), F32)],
        compiler_params=_cparams(("arbitrary",)),
        name="inproj",
    )(x2, norm_mix, w_in, conv_w, conv_b)


def _att_block(qh, kj, vj, tri, c, o, mask):
    z = lax.dot_general(qh, kj, (((1,), (1,)), ((), ())), preferred_element_type=F32)
    sp = jnp.maximum(z, 0.0) + jnp.log(1.0 + jnp.exp(-jnp.abs(z)))
    l1mb = -sp
    if mask is not None:
        l1mb = jnp.where(mask, l1mb, 0.0)
    hi = l1mb.astype(BF16)
    lo = (l1mb - hi.astype(F32)).astype(BF16)
    excl = (jnp.dot(hi, tri, preferred_element_type=F32)
            + jnp.dot(lo, tri, preferred_element_type=F32))
    a = jnp.exp((z - sp) + excl + c)
    if mask is not None:
        a = jnp.where(mask, a, 0.0)
    o = o + jnp.dot(a.astype(BF16), vj, preferred_element_type=F32)
    c = c + jnp.sum(l1mb, axis=-1, keepdims=True)
    return c, o


def _attention_kernel(q_ref, k_ref, v_ref, o_ref, *, tq, tk, scale):
    i = pl.program_id(2)
    q = q_ref[...] * jnp.asarray(scale, BF16)
    lane = lax.broadcasted_iota(jnp.int32, q.shape, 1)
    kr = lax.broadcasted_iota(jnp.int32, (tk, tk), 0)
    kc = lax.broadcasted_iota(jnp.int32, (tk, tk), 1)
    tri = jnp.where(kr > kc, 1.0, 0.0).astype(BF16)
    qr = lax.broadcasted_iota(jnp.int32, (tq, tk), 0)
    qc = lax.broadcasted_iota(jnp.int32, (tq, tk), 1)
    diag_mask = qc < qr

    zero = jnp.zeros_like(q)
    qs = (jnp.where(lane < SB_HEAD_DIM, q, zero), jnp.where(lane >= SB_HEAD_DIM, q, zero))

    def both_heads(j, cs, os_, mask):
        s = pl.multiple_of(j * tk, tk)
        kj = k_ref[pl.ds(s, tk), :]
        vj = v_ref[pl.ds(s, tk), :]
        new = [_att_block(qs[h], kj, vj, tri, cs[h], os_[h], mask) for h in range(2)]
        return (new[0][0], new[1][0]), (new[0][1], new[1][1])

    def alive(cs):
        return (jnp.max(jnp.maximum(cs[0], cs[1])) >= EXP_UNDERFLOW_LOG).astype(jnp.int32)

    def write(os_):
        o_ref[...] = jnp.where(lane < SB_HEAD_DIM, os_[0], os_[1]).astype(o_ref.dtype)

    c0 = jnp.zeros((tq, 1), F32)
    o0 = jnp.zeros((tq, LANES), F32)

    @pl.when(i == 0)
    def _():
        _, os_ = both_heads(i, (c0, c0), (o0, o0), diag_mask)
        write(os_)

    @pl.when(i > 0)
    def _():
        cs, os_ = both_heads(i, (c0, c0), (o0, o0), diag_mask)
        cs, os_ = both_heads(i - 1, cs, os_, None)

        def cond(state):
            jj, _, _, live = state
            return jnp.logical_and(jj <= i, live > 0)

        def body(state):
            jj, cs, os_, _ = state
            cs, os_ = both_heads(i - jj, cs, os_, None)
            return jj + 1, cs, os_, alive(cs)

        _, _, os_, _ = lax.while_loop(cond, body, (jnp.int32(2), cs, os_, alive(cs)))
        write(os_)


def _attention(qkv, batch, seq):
    n = qkv.shape[0]
    tq, tk = ATT_TQ, ATT_TK
    assert tq == tk
    pairs = SB_WIDTH // LANES
    nq = seq // tq
    kern = functools.partial(_attention_kernel, tq=tq, tk=tk, scale=SB_HEAD_DIM ** -0.5)
    return pl.pallas_call(
        kern,
        grid=(batch, pairs, nq),
        in_specs=[
            pl.BlockSpec((tq, LANES), lambda b, p, i: (b * nq + i, p)),
            pl.BlockSpec((seq, LANES), lambda b, p, i: (b, pairs + p)),
            pl.BlockSpec((seq, LANES), lambda b, p, i: (b, 2 * pairs + p)),
        ],
        out_specs=pl.BlockSpec((tq, LANES), lambda b, p, i: (b * nq + i, p)),
        out_shape=jax.ShapeDtypeStruct((n, SB_WIDTH), BF16),
        compiler_params=_cparams(("arbitrary", "arbitrary", "arbitrary")),
        name="stickbreak_attention",
    )(qkv, qkv, qkv)


def _mid_kernel(att_ref, cbo_ref, gate_ref, x_ref, wa_ref, wb_ref, wo_ref, nf_ref, wqt_ref, keys_ref,
                x1_ref, xnt_ref, st_ref):
    d = x_ref.shape[1]
    ya = jnp.dot(att_ref[...], wa_ref[...], preferred_element_type=F32)
    yb = jnp.dot(cbo_ref[...], wb_ref[...], preferred_element_type=F32)
    merged = (gate_ref[:, 0:d] * ya + gate_ref[:, d:2 * d] * yb).astype(BF16)
    x1 = x_ref[...] + jnp.dot(merged, wo_ref[...], preferred_element_type=F32)
    x1_ref[...] = x1
    ms = jnp.mean(x1 * x1, axis=-1, keepdims=True)
    xn = x1 * lax.rsqrt(ms + RMS_EPS) * nf_ref[...]
    xnt_ref[...] = xn.T.astype(BF16)
    qt = lax.dot_general(wqt_ref[...], xn.astype(BF16), (((1,), (1,)), ((), ())),
                         preferred_element_type=F32)
    for hp in range(st_ref.shape[0]):
        qhp = qt[hp * N_KEYS:(hp + 1) * N_KEYS, :].astype(BF16)
        st_ref[hp] = jnp.dot(keys_ref[hp], qhp, preferred_element_type=F32)


def _mid(att, cbo, gates, x2, wa, wb, wo, norm_ffn, wqt, keys):
    n, d = x2.shape
    tm = MID_TM
    nhp = keys.shape[0]
    const2 = lambda i: (0, 0)
    return pl.pallas_call(
        _mid_kernel,
        grid=(n // tm,),
        in_specs=[
            pl.BlockSpec((tm, SB_WIDTH), lambda i: (i, 0)),
            pl.BlockSpec((tm, SB_WIDTH), lambda i: (i, 0)),
            pl.BlockSpec((tm, 2 * d), lambda i: (i, 0)),
            pl.BlockSpec((tm, d), lambda i: (i, 0)),
            pl.BlockSpec(wa.shape, const2),
            pl.BlockSpec(wb.shape, const2),
            pl.BlockSpec(wo.shape, const2),
            pl.BlockSpec((1, d), const2),
            pl.BlockSpec(wqt.shape, const2),
            pl.BlockSpec(keys.shape, lambda i: (0, 0, 0)),
        ],
        out_specs=[
            pl.BlockSpec((tm, d), lambda i: (i, 0)),
            pl.BlockSpec((d, tm), lambda i: (0, i)),
            pl.BlockSpec((nhp, N_KEYS, tm), lambda i: (0, 0, i)),
        ],
        out_shape=[
            jax.ShapeDtypeStruct((n, d), F32),
            jax.ShapeDtypeStruct((d, n), BF16),
            jax.ShapeDtypeStruct((nhp, N_KEYS, n), F32),
        ],
        compiler_params=_cparams(("arbitrary",)),
        name="mid_proj",
    )(att, cbo, gates, x2, wa, wb, wo, norm_ffn, wqt, keys)


def _candidate_cells():
    return [(a, b) for a in range(PEER_TOPK) for b in range(PEER_TOPK) if (a + 1) * (b + 1) <= PEER_TOPK]


def _topk_kernel(st_ref, g1_ref, n1_ref, g2_ref, r2_ref, rank_scr, tops_scr):
    k = PEER_TOPK
    neg_inf = jnp.asarray(-jnp.inf, F32)
    kio = lax.broadcasted_iota(jnp.int32, (N_KEYS, LANES), 0).astype(F32)
    sub = lax.broadcasted_iota(jnp.int32, (SUBLANES, LANES), 0)

    def record_top(p, h, r, m):
        slot = p * k + r
        tops_scr[slot] = jnp.where(sub == h, m, tops_scr[slot])

    def extract_all_equal(h, tied):
        for p in range(2):
            x = st_ref[2 * h + p]
            rank = jnp.full((N_KEYS, LANES), float(k), F32)
            for r in range(k):
                m = jnp.max(x, axis=0, keepdims=True)
                hit = x == m
                rank = jnp.where(hit, float(r), rank)
                x = jnp.where(hit, neg_inf, x)
                record_top(p, h, r, m)
            rank_scr[2 * h + p] = rank
            removed = jnp.sum(jnp.where(rank < float(k), 1.0, 0.0), axis=0, keepdims=True)
            tied = jnp.maximum(tied, jnp.where(removed != float(k), 1.0, 0.0))
        return tied

    tied = lax.fori_loop(0, PEER_HEADS, extract_all_equal, jnp.zeros((1, LANES), F32))

    @pl.when(jnp.max(tied) > 0.0)
    def _():
        def extract_first(hp, _):
            x = st_ref[hp]
            rank = jnp.full((N_KEYS, LANES), float(k), F32)
            h = hp // 2
            p = hp % 2
            for r in range(k):
                m = jnp.max(x, axis=0, keepdims=True)
                first = jnp.min(jnp.where(x == m, kio, float(N_KEYS)), axis=0, keepdims=True)
                hit = kio == first
                rank = jnp.where(hit, float(r), rank)
                x = jnp.where(hit, neg_inf, x)
                record_top(p, h, r, m)
            rank_scr[hp] = rank
            return 0

        lax.fori_loop(0, st_ref.shape[0], extract_first, 0)

    t1 = [tops_scr[a] for a in range(k)]
    t2 = [tops_scr[k + b] for b in range(k)]
    cells = _candidate_cells()
    val = {ab: t1[ab[0]] + t2[ab[1]] for ab in cells}
    beat = {ab: jnp.full((SUBLANES, LANES), float((ab[0] + 1) * (ab[1] + 1) - 1), F32) for ab in cells}
    for ip, p_ in enumerate(cells):
        for q_ in cells[ip + 1:]:
            if p_[0] <= q_[0] and p_[1] <= q_[1]:
                continue
            t = jnp.where(val[p_] >= val[q_], 1.0, 0.0)
            beat[q_] = beat[q_] + t
            beat[p_] = beat[p_] + (1.0 - t)
    top = val[(0, 0)]
    zsum = jnp.zeros((SUBLANES, LANES), F32)
    n_sel = [jnp.zeros((SUBLANES, LANES), F32) for _ in range(k)]
    for ab in cells:
        sel = beat[ab] < float(k)
        zsum = zsum + jnp.where(sel, jnp.exp(val[ab] - top), 0.0)
        n_sel[ab[0]] = n_sel[ab[0]] + jnp.where(sel, 1.0, 0.0)
    zinv = 1.0 / zsum

    for h in range(PEER_HEADS):
        rank1 = rank_scr[2 * h]
        rank2 = rank_scr[2 * h + 1]
        m1 = t1[0][h:h + 1, :]
        m2 = t2[0][h:h + 1, :]
        g1_ref[h] = jnp.where(rank1 < float(k), jnp.exp(st_ref[2 * h] - m1), 0.0)
        g2 = jnp.where(rank2 < float(k), jnp.exp(st_ref[2 * h + 1] - m2) * zinv[h:h + 1, :], 0.0)
        g2_ref[h] = g2.astype(BF16)
        r2_ref[h] = rank2.astype(BF16)
        n1 = jnp.zeros((N_KEYS, LANES), F32)
        for a in range(k):
            n1 = jnp.where(rank1 == float(a), n_sel[a][h:h + 1, :], n1)
        n1_ref[h] = n1


def _topk(st):
    nhp, nk, n = st.shape
    tn = TOPK_TN
    heads = nhp // 2
    spec = pl.BlockSpec((heads, nk, tn), lambda i: (0, 0, i))
    return pl.pallas_call(
        _topk_kernel,
        grid=(n // tn,),
        in_specs=[pl.BlockSpec((nhp, nk, tn), lambda i: (0, 0, i))],
        out_specs=[spec, spec, spec, spec],
        out_shape=[
            jax.ShapeDtypeStruct((heads, nk, n), F32),
            jax.ShapeDtypeStruct((heads, nk, n), F32),
            jax.ShapeDtypeStruct((heads, nk, n), BF16),
            jax.ShapeDtypeStruct((heads, nk, n), BF16),
        ],
        scratch_shapes=[
            pltpu.VMEM((nhp, nk, tn), F32),
            pltpu.VMEM((2 * PEER_TOPK, SUBLANES, tn), F32),
        ],
        compiler_params=_cparams(("arbitrary",)),
        name="peer_topk",
    )(st)


def _peer_kernel(xnt_ref, u_ref, vt_ref, g1_ref, n1_ref, g2_ref, r2_ref, x1_ref, fn_ref, out_ref,
                 s_scr, p_scr, acc_scr):
    e = pl.program_id(1)
    te = u_ref.shape[0]

    @pl.when(e == 0)
    def _():
        acc_scr[...] = jnp.zeros_like(acc_scr)

    s_scr[...] = jnp.dot(u_ref[...], xnt_ref[...], preferred_element_type=F32)

    def i1_body(j, _):
        rows = pl.ds(pl.multiple_of(j * N_KEYS, N_KEYS), N_KEYS)
        s = s_scr[rows, :]
        act = 0.5 * s * (1.0 + lax.erf(s * math.sqrt(0.5)))
        w = jnp.zeros(s.shape, BF16)
        for h in range(PEER_HEADS):
            g1 = g1_ref[h, pl.ds(j, 1), :].astype(BF16)
            n1 = n1_ref[h, pl.ds(j, 1), :].astype(BF16)
            w = w + jnp.where(r2_ref[h] < n1, g2_ref[h] * g1, jnp.zeros_like(w))
        p_scr[rows, :] = act.astype(BF16) * w
        return 0

    lax.fori_loop(0, te // N_KEYS, i1_body, 0)
    acc_scr[...] += jnp.dot(vt_ref[...], p_scr[...], preferred_element_type=F32)

    @pl.when(e == pl.num_programs(1) - 1)
    def _():
        y = x1_ref[...] + acc_scr[...].T
        ms = jnp.mean(y * y, axis=-1, keepdims=True)
        out_ref[...] = y * lax.rsqrt(ms + RMS_EPS) * fn_ref[...]


def _peer(xnt, u, vt, g1, n1, g2, r2, x1, final_norm):
    d, n = xnt.shape
    n_exp = u.shape[0]
    tn, te = PEER_TN, PEER_TE
    heads = g1.shape[0]
    i1_per_tile = te // N_KEYS
    assert i1_per_tile == SUBLANES
    return pl.pallas_call(
        _peer_kernel,
        grid=(n // tn, n_exp // te),
        in_specs=[
            pl.BlockSpec((d, tn), lambda t, e: (0, t)),
            pl.BlockSpec((te, d), lambda t, e: (e, 0)),
            pl.BlockSpec((d, te), lambda t, e: (0, e)),
            pl.BlockSpec((heads, i1_per_tile, tn), lambda t, e: (0, e, t)),
            pl.BlockSpec((heads, i1_per_tile, tn), lambda t, e: (0, e, t)),
            pl.BlockSpec((heads, N_KEYS, tn), lambda t, e: (0, 0, t)),
            pl.BlockSpec((heads, N_KEYS, tn), lambda t, e: (0, 0, t)),
            pl.BlockSpec((tn, d), lambda t, e: (t, 0)),
            pl.BlockSpec((1, d), lambda t, e: (0, 0)),
        ],
        out_specs=pl.BlockSpec((tn, d), lambda t, e: (t, 0)),
        out_shape=jax.ShapeDtypeStruct((n, d), F32),
        scratch_shapes=[
            pltpu.VMEM((te, tn), F32),
            pltpu.VMEM((te, tn), BF16),
            pltpu.VMEM((d, tn), F32),
        ],
        compiler_params=_cparams(("arbitrary", "arbitrary")),
        name="peer_dense",
    )(xnt, u, vt, g1, n1, g2, r2, x1, final_norm)


def _layer(x2, batch, seq, norm_mix, w_in, conv_w, conv_b, w_a, w_b, w_o, norm_ffn, w_q, sub_keys,
           expert_u, expert_v, out_norm):
    d = x2.shape[1]
    qkv, cbo, gates = _inproj(x2, norm_mix.reshape(1, d), w_in.astype(BF16),
                              conv_w.reshape(CONV_K, -1), conv_b.reshape(1, -1), seq)
    att = _attention(qkv, batch, seq)
    keys = sub_keys.reshape(-1, N_KEYS, sub_keys.shape[-1]).astype(BF16)
    x1, xnt, st = _mid(att, cbo, gates, x2, w_a.astype(BF16), w_b.astype(BF16), w_o.astype(BF16),
                       norm_ffn.reshape(1, d), w_q.T.astype(BF16), keys)
    g1, n1, g2, r2 = _topk(st)
    return _peer(xnt, expert_u.astype(BF16), expert_v.T.astype(BF16), g1, n1, g2, r2, x1, out_norm)


def kernel(x, norm_mix, w_in, conv_w, conv_b, w_branch_a, w_branch_b, w_out, norm_ffn, w_q, sub_keys,
           expert_u, expert_v, final_norm):
    batch, seq, d = x.shape
    depth = w_in.shape[0]
    assert depth == 1, "the final RMSNorm is fused into the last layer's PEER kernel"
    x2 = x.reshape(batch * seq, d)
    out = _layer(x2, batch, seq, norm_mix[0], w_in[0], conv_w[0], conv_b[0], w_branch_a[0],
                 w_branch_b[0], w_out[0], norm_ffn[0], w_q[0], sub_keys[0], expert_u[0], expert_v[0],
                 final_norm.reshape(1, d))
    return out.reshape(batch, seq, d)
```

```python
import functools
import math

import jax
import jax.numpy as jnp
from jax import lax
from jax.experimental import pallas as pl
from jax.experimental.pallas import tpu as pltpu

F32 = jnp.float32
BF16 = jnp.bfloat16

RMS_EPS = 1e-6
SB_HEADS = 8
SB_HEAD_DIM = 64
SB_WIDTH = SB_HEADS * SB_HEAD_DIM
CONV_K = 3
PEER_HEADS = 8
N_KEYS = 128
PEER_TOPK = 16
LANES = 128
SUBLANES = 8
VMEM_LIMIT_BYTES = 56 * 1024 * 1024
EXP_UNDERFLOW_LOG = -110.0

IN_TM = 256
ATT_TQ = 256
ATT_TK = 256
MID_TM = 256
TOPK_TN = 128
PEER_TN = 512
PEER_TE = 1024
MXU_DIM = 256
PEER_MM_ROWS = 256


def _cparams(sem):
    return pltpu.CompilerParams(dimension_semantics=sem, vmem_limit_bytes=VMEM_LIMIT_BYTES)


def _inproj_kernel(x_ref, nm_ref, w_ref, cw_ref, cb_ref, qkv_ref, cbo_ref, gate_ref, carry_ref,
                   *, tiles_per_seq, width):
    i = pl.program_id(0)
    x = x_ref[...]
    ms = jnp.mean(x * x, axis=-1, keepdims=True)
    h = (x * lax.rsqrt(ms + RMS_EPS) * nm_ref[...]).astype(BF16)
    o3 = 3 * width
    qkv_ref[...] = jnp.dot(h, w_ref[:, 0:o3], preferred_element_type=F32).astype(BF16)

    c = jnp.dot(h, w_ref[:, o3:2 * o3], preferred_element_type=F32)
    cb, cc, cu = c[:, 0:width], c[:, width:2 * width], c[:, 2 * width:3 * width]
    z = cc * cu
    tm = z.shape[0]

    @pl.when(i % tiles_per_seq == 0)
    def _():
        carry_ref[...] = jnp.zeros_like(carry_ref)

    prev = carry_ref[...]
    rows = lax.broadcasted_iota(jnp.int32, z.shape, 0)
    z1 = jnp.where(rows == 0, prev[7:8, :], pltpu.roll(z, 1, 0))
    z2 = jnp.where(rows == 0, prev[6:7, :], jnp.where(rows == 1, prev[7:8, :], pltpu.roll(z, 2, 0)))
    y = cw_ref[0:1, :] * z2 + cw_ref[1:2, :] * z1 + cw_ref[2:3, :] * z + cb_ref[...]
    cbo_ref[...] = (cb * y).astype(BF16)
    carry_ref[...] = z[tm - SUBLANES:, :]

    g = jnp.dot(h, w_ref[:, 2 * o3:], preferred_element_type=F32)
    gate_ref[...] = jax.nn.sigmoid(g)


def _inproj(x2, norm_mix, w_in, conv_w, conv_b, seq):
    n, d = x2.shape
    width = SB_WIDTH
    in_width = w_in.shape[1]
    tm = IN_TM
    kern = functools.partial(_inproj_kernel, tiles_per_seq=seq // tm, width=width)
    return pl.pallas_call(
        kern,
        grid=(n // tm,),
        in_specs=[
            pl.BlockSpec((tm, d), lambda i: (i, 0)),
            pl.BlockSpec((1, d), lambda i: (0, 0)),
            pl.BlockSpec((d, in_width), lambda i: (0, 0)),
            pl.BlockSpec((CONV_K, width), lambda i: (0, 0)),
            pl.BlockSpec((1, width), lambda i: (0, 0)),
        ],
        out_specs=[
            pl.BlockSpec((tm, 3 * width), lambda i: (i, 0)),
            pl.BlockSpec((tm, width), lambda i: (i, 0)),
            pl.BlockSpec((tm, 2 * d), lambda i: (i, 0)),
        ],
        out_shape=[
            jax.ShapeDtypeStruct((n, 3 * width), BF16),
            jax.ShapeDtypeStruct((n, width), BF16),
            jax.ShapeDtypeStruct((n, 2 * d), F32),
        ],
        scratch_shapes=[pltpu.VMEM((SUBLANES, width), F32)],
        compiler_params=_cparams(("arbitrary",)),
        name="inproj",
    )(x2, norm_mix, w_in, conv_w, conv_b)


def _att_block(qh, kj, vj, tri, c, o, mask):
    z = lax.dot_general(qh, kj, (((1,), (1,)), ((), ())), preferred_element_type=F32)
    sp = jnp.maximum(z, 0.0) + jnp.log(1.0 + jnp.exp(-jnp.abs(z)))
    l1mb = -sp
    if mask is not None:
        l1mb = jnp.where(mask, l1mb, 0.0)
    hi = l1mb.astype(BF16)
    lo = (l1mb - hi.astype(F32)).astype(BF16)
    excl = (jnp.dot(hi, tri, preferred_element_type=F32)
            + jnp.dot(lo, tri, preferred_element_type=F32))
    a = jnp.exp((z - sp) + excl + c)
    if mask is not None:
        a = jnp.where(mask, a, 0.0)
    o = o + jnp.dot(a.astype(BF16), vj, preferred_element_type=F32)
    c = c + jnp.sum(l1mb, axis=-1, keepdims=True)
    return c, o


def _attention_kernel(q_ref, k_ref, v_ref, o_ref, *, tq, tk, scale):
    i = pl.program_id(2)
    q = q_ref[...] * jnp.asarray(scale, BF16)
    lane = lax.broadcasted_iota(jnp.int32, q.shape, 1)
    kr = lax.broadcasted_iota(jnp.int32, (tk, tk), 0)
    kc = lax.broadcasted_iota(jnp.int32, (tk, tk), 1)
    tri = jnp.where(kr > kc, 1.0, 0.0).astype(BF16)
    qr = lax.broadcasted_iota(jnp.int32, (tq, tk), 0)
    qc = lax.broadcasted_iota(jnp.int32, (tq, tk), 1)
    diag_mask = qc < qr

    zero = jnp.zeros_like(q)
    qs = (jnp.where(lane < SB_HEAD_DIM, q, zero), jnp.where(lane >= SB_HEAD_DIM, q, zero))

    def both_heads(j, cs, os_, mask):
        s = pl.multiple_of(j * tk, tk)
        kj = k_ref[pl.ds(s, tk), :]
        vj = v_ref[pl.ds(s, tk), :]
        new = [_att_block(qs[h], kj, vj, tri, cs[h], os_[h], mask) for h in range(2)]
        return (new[0][0], new[1][0]), (new[0][1], new[1][1])

    def alive(cs):
        return (jnp.max(jnp.maximum(cs[0], cs[1])) >= EXP_UNDERFLOW_LOG).astype(jnp.int32)

    def write(os_):
        o_ref[...] = jnp.where(lane < SB_HEAD_DIM, os_[0], os_[1]).astype(o_ref.dtype)

    c0 = jnp.zeros((tq, 1), F32)
    o0 = jnp.zeros((tq, LANES), F32)

    @pl.when(i == 0)
    def _():
        _, os_ = both_heads(i, (c0, c0), (o0, o0), diag_mask)
        write(os_)

    @pl.when(i > 0)
    def _():
        cs, os_ = both_heads(i, (c0, c0), (o0, o0), diag_mask)
        cs, os_ = both_heads(i - 1, cs, os_, None)

        def cond(state):
            jj, _, _, live = state
            return jnp.logical_and(jj <= i, live > 0)

        def body(state):
            jj, cs, os_, _ = state
            cs, os_ = both_heads(i - jj, cs, os_, None)
            return jj + 1, cs, os_, alive(cs)

        _, _, os_, _ = lax.while_loop(cond, body, (jnp.int32(2), cs, os_, alive(cs)))
        write(os_)


def _attention(qkv, batch, seq):
    n = qkv.shape[0]
    tq, tk = ATT_TQ, ATT_TK
    assert tq == tk
    pairs = SB_WIDTH // LANES
    nq = seq // tq
    kern = functools.partial(_attention_kernel, tq=tq, tk=tk, scale=SB_HEAD_DIM ** -0.5)
    return pl.pallas_call(
        kern,
        grid=(batch, pairs, nq),
        in_specs=[
            pl.BlockSpec((tq, LANES), lambda b, p, i: (b * nq + i, p)),
            pl.BlockSpec((seq, LANES), lambda b, p, i: (b, pairs + p)),
            pl.BlockSpec((seq, LANES), lambda b, p, i: (b, 2 * pairs + p)),
        ],
        out_specs=pl.BlockSpec((tq, LANES), lambda b, p, i: (b * nq + i, p)),
        out_shape=jax.ShapeDtypeStruct((n, SB_WIDTH), BF16),
        compiler_params=_cparams(("arbitrary", "arbitrary", "arbitrary")),
        name="stickbreak_attention",
    )(qkv, qkv, qkv)


def _mid_kernel(att_ref, cbo_ref, gate_ref, x_ref, wa_ref, wb_ref, wo_ref, nf_ref, wqt_ref, keys_ref,
                x1_ref, xnt_ref, st_ref):
    d = x_ref.shape[1]
    ya = jnp.dot(att_ref[...], wa_ref[...], preferred_element_type=F32)
    yb = jnp.dot(cbo_ref[...], wb_ref[...], preferred_element_type=F32)
    merged = (gate_ref[:, 0:d] * ya + gate_ref[:, d:2 * d] * yb).astype(BF16)
    x1 = x_ref[...] + jnp.dot(merged, wo_ref[...], preferred_element_type=F32)
    x1_ref[...] = x1
    ms = jnp.mean(x1 * x1, axis=-1, keepdims=True)
    xn = x1 * lax.rsqrt(ms + RMS_EPS) * nf_ref[...]
    xnt_ref[...] = xn.T.astype(BF16)
    qt = lax.dot_general(wqt_ref[...], xn.astype(BF16), (((1,), (1,)), ((), ())),
                         preferred_element_type=F32)
    for hp in range(st_ref.shape[0]):
        qhp = qt[hp * N_KEYS:(hp + 1) * N_KEYS, :].astype(BF16)
        st_ref[hp] = jnp.dot(keys_ref[hp], qhp, preferred_element_type=F32)


def _mid(att, cbo, gates, x2, wa, wb, wo, norm_ffn, wqt, keys):
    n, d = x2.shape
    tm = MID_TM
    nhp = keys.shape[0]
    const2 = lambda i: (0, 0)
    return pl.pallas_call(
        _mid_kernel,
        grid=(n // tm,),
        in_specs=[
            pl.BlockSpec((tm, SB_WIDTH), lambda i: (i, 0)),
            pl.BlockSpec((tm, SB_WIDTH), lambda i: (i, 0)),
            pl.BlockSpec((tm, 2 * d), lambda i: (i, 0)),
            pl.BlockSpec((tm, d), lambda i: (i, 0)),
            pl.BlockSpec(wa.shape, const2),
            pl.BlockSpec(wb.shape, const2),
            pl.BlockSpec(wo.shape, const2),
            pl.BlockSpec((1, d), const2),
            pl.BlockSpec(wqt.shape, const2),
            pl.BlockSpec(keys.shape, lambda i: (0, 0, 0)),
        ],
        out_specs=[
            pl.BlockSpec((tm, d), lambda i: (i, 0)),
            pl.BlockSpec((d, tm), lambda i: (0, i)),
            pl.BlockSpec((nhp, N_KEYS, tm), lambda i: (0, 0, i)),
        ],
        out_shape=[
            jax.ShapeDtypeStruct((n, d), F32),
            jax.ShapeDtypeStruct((d, n), BF16),
            jax.ShapeDtypeStruct((nhp, N_KEYS, n), F32),
        ],
        compiler_params=_cparams(("arbitrary",)),
        name="mid_proj",
    )(att, cbo, gates, x2, wa, wb, wo, norm_ffn, wqt, keys)


def _candidate_cells():
    return [(a, b) for a in range(PEER_TOPK) for b in range(PEER_TOPK) if (a + 1) * (b + 1) <= PEER_TOPK]


def _topk_kernel(st_ref, g1_ref, n1_ref, g2_ref, r2_ref, rank_scr, tops_scr):
    k = PEER_TOPK
    neg_inf = jnp.asarray(-jnp.inf, F32)
    kio = lax.broadcasted_iota(jnp.int32, (N_KEYS, LANES), 0).astype(F32)
    sub = lax.broadcasted_iota(jnp.int32, (SUBLANES, LANES), 0)

    def record_top(p, h, r, m):
        slot = p * k + r
        tops_scr[slot] = jnp.where(sub == h, m, tops_scr[slot])

    def extract_all_equal(h, tied):
        for p in range(2):
            x = st_ref[2 * h + p]
            rank = jnp.full((N_KEYS, LANES), float(k), F32)
            for r in range(k):
                m = jnp.max(x, axis=0, keepdims=True)
                hit = x == m
                rank = jnp.where(hit, float(r), rank)
                x = jnp.where(hit, neg_inf, x)
                record_top(p, h, r, m)
            rank_scr[2 * h + p] = rank
            removed = jnp.sum(jnp.where(rank < float(k), 1.0, 0.0), axis=0, keepdims=True)
            tied = jnp.maximum(tied, jnp.where(removed != float(k), 1.0, 0.0))
        return tied

    tied = lax.fori_loop(0, PEER_HEADS, extract_all_equal, jnp.zeros((1, LANES), F32))

    @pl.when(jnp.max(tied) > 0.0)
    def _():
        def extract_first(hp, _):
            x = st_ref[hp]
            rank = jnp.full((N_KEYS, LANES), float(k), F32)
            h = hp // 2
            p = hp % 2
            for r in range(k):
                m = jnp.max(x, axis=0, keepdims=True)
                first = jnp.min(jnp.where(x == m, kio, float(N_KEYS)), axis=0, keepdims=True)
                hit = kio == first
                rank = jnp.where(hit, float(r), rank)
                x = jnp.where(hit, neg_inf, x)
                record_top(p, h, r, m)
            rank_scr[hp] = rank
            return 0

        lax.fori_loop(0, st_ref.shape[0], extract_first, 0)

    t1 = [tops_scr[a] for a in range(k)]
    t2 = [tops_scr[k + b] for b in range(k)]
    cells = _candidate_cells()
    val = {ab: t1[ab[0]] + t2[ab[1]] for ab in cells}
    beat = {ab: jnp.full((SUBLANES, LANES), float((ab[0] + 1) * (ab[1] + 1) - 1), F32) for ab in cells}
    for ip, p_ in enumerate(cells):
        for q_ in cells[ip + 1:]:
            if p_[0] <= q_[0] and p_[1] <= q_[1]:
                continue
            t = jnp.where(val[p_] >= val[q_], 1.0, 0.0)
            beat[q_] = beat[q_] + t
            beat[p_] = beat[p_] + (1.0 - t)
    top = val[(0, 0)]
    zsum = jnp.zeros((SUBLANES, LANES), F32)
    n_sel = [jnp.zeros((SUBLANES, LANES), F32) for _ in range(k)]
    for ab in cells:
        sel = beat[ab] < float(k)
        zsum = zsum + jnp.where(sel, jnp.exp(val[ab] - top), 0.0)
        n_sel[ab[0]] = n_sel[ab[0]] + jnp.where(sel, 1.0, 0.0)
    zinv = 1.0 / zsum

    for h in range(PEER_HEADS):
        rank1 = rank_scr[2 * h]
        rank2 = rank_scr[2 * h + 1]
        m1 = t1[0][h:h + 1, :]
        m2 = t2[0][h:h + 1, :]
        g1_ref[h] = jnp.where(rank1 < float(k), jnp.exp(st_ref[2 * h] - m1), 0.0)
        g2 = jnp.where(rank2 < float(k), jnp.exp(st_ref[2 * h + 1] - m2) * zinv[h:h + 1, :], 0.0)
        g2_ref[h, 0] = g2.astype(BF16)
        r2_ref[h, 0] = rank2.astype(BF16)
        n1 = jnp.zeros((N_KEYS, LANES), F32)
        for a in range(k):
            n1 = jnp.where(rank1 == float(a), n_sel[a][h:h + 1, :], n1)
        n1_ref[h] = n1


def _topk(st):
    nhp, nk, n = st.shape
    tn = TOPK_TN
    heads = nhp // 2
    assert tn == LANES
    spec = pl.BlockSpec((heads, nk, tn), lambda i: (0, 0, i))
    spec2 = pl.BlockSpec((heads, 1, nk, tn), lambda i: (0, i, 0, 0))
    return pl.pallas_call(
        _topk_kernel,
        grid=(n // tn,),
        in_specs=[pl.BlockSpec((nhp, nk, tn), lambda i: (0, 0, i))],
        out_specs=[spec, spec, spec2, spec2],
        out_shape=[
            jax.ShapeDtypeStruct((heads, nk, n), F32),
            jax.ShapeDtypeStruct((heads, nk, n), F32),
            jax.ShapeDtypeStruct((heads, n // tn, nk, tn), BF16),
            jax.ShapeDtypeStruct((heads, n // tn, nk, tn), BF16),
        ],
        scratch_shapes=[
            pltpu.VMEM((nhp, nk, tn), F32),
            pltpu.VMEM((2 * PEER_TOPK, SUBLANES, tn), F32),
        ],
        compiler_params=_cparams(("arbitrary",)),
        name="peer_topk",
    )(st)


def _peer_tick(xnt_ref, u_ref, vt_ref, g1_ref, n1_ref, g2_ref, r2_ref, acc_scr,
               s_write, s_read, p_write, p_read, first_of_token_tile):
    te, tn = s_read.shape
    d = vt_ref.shape[0]
    mb = MXU_DIM
    rb = PEER_MM_ROWS
    a_chunks = [(r, c) for c in range(tn // mb) for r in range(te // rb)]
    c_chunks = [(m, c) for c in range(tn // mb) for m in range(d // rb)]
    n_blocks = te // N_KEYS
    every = n_blocks // len(a_chunks)
    assert len(a_chunks) * every == n_blocks and len(c_chunks) == len(a_chunks)
    lanes_per_mb = mb // LANES

    def scores(j):
        r, c = a_chunks[j]
        rows, cols = slice(r * rb, (r + 1) * rb), slice(c * mb, (c + 1) * mb)
        s_write[rows, cols] = jnp.dot(u_ref[rows, :], xnt_ref[:, cols], preferred_element_type=F32)

    def gates(j, lc):
        rows = slice(j * N_KEYS, (j + 1) * N_KEYS)
        cols = slice(lc * mb, (lc + 1) * mb)
        groups = range(lc * lanes_per_mb, (lc + 1) * lanes_per_mb)
        w = jnp.zeros((N_KEYS, mb), BF16)
        for h in range(PEER_HEADS):
            g1 = g1_ref[h, j:j + 1, cols].astype(BF16)
            n1 = n1_ref[h, j:j + 1, cols].astype(BF16)
            g2 = jnp.concatenate([g2_ref[h, g] for g in groups], axis=1)
            r2 = jnp.concatenate([r2_ref[h, g] for g in groups], axis=1)
            w = w + jnp.where(r2 < n1, g2, jnp.zeros_like(g2)) * g1
        s = s_read[rows, cols]
        act = 0.5 * s * (1.0 + lax.erf(s * math.sqrt(0.5)))
        p = act.astype(BF16) * w
        for i, g in enumerate(groups):
            p_write[g, rows, :] = p[:, i * LANES:(i + 1) * LANES]

    def contract(j):
        m, c = c_chunks[j]
        rows, cols = slice(m * rb, (m + 1) * rb), slice(c * mb, (c + 1) * mb)
        p_cols = jnp.concatenate([p_read[c * lanes_per_mb + i] for i in range(lanes_per_mb)], axis=1)
        upd = jnp.dot(vt_ref[rows, :], p_cols, preferred_element_type=F32)
        acc_scr[rows, cols] = jnp.where(first_of_token_tile, upd, acc_scr[rows, cols] + upd)

    for j in range(n_blocks):
        gates(j, 0)
        if j % every == 0:
            scores(j // every)
        gates(j, 1)
        if j % every == every - 1:
            contract(j // every)


def _peer_kernel(xnt_ref, u_ref, vt_ref, g1_ref, n1_ref, g2_ref, r2_ref, x1_ref, fn_ref, out_ref,
                 s0, s1, p0, p1, acc_scr, *, n_items, e_tiles):
    k = pl.program_id(0)
    c_item = jnp.clip(k - 2, 0, n_items - 1)
    e_c = c_item % e_tiles

    @pl.when(k == 0)
    def _():
        s1[...] = jnp.zeros_like(s1)
        p0[...] = jnp.zeros_like(p0)
        acc_scr[...] = jnp.zeros_like(acc_scr)

    refs = (xnt_ref, u_ref, vt_ref, g1_ref, n1_ref, g2_ref, r2_ref, acc_scr)

    @pl.when(k % 2 == 0)
    def _():
        _peer_tick(*refs, s0, s1, p1, p0, e_c == 0)

    @pl.when(k % 2 == 1)
    def _():
        _peer_tick(*refs, s1, s0, p0, p1, e_c == 0)

    @pl.when(jnp.logical_and(k >= 2, e_c == e_tiles - 1))
    def _():
        y = x1_ref[...] + acc_scr[...].T
        ms = jnp.mean(y * y, axis=-1, keepdims=True)
        out_ref[...] = y * lax.rsqrt(ms + RMS_EPS) * fn_ref[...]


def _peer(xnt, u, vt, g1, n1, g2, r2, x1, final_norm):
    d, n = xnt.shape
    n_exp = u.shape[0]
    tn, te = PEER_TN, PEER_TE
    heads = g1.shape[0]
    i1_per_tile = te // N_KEYS
    assert i1_per_tile == SUBLANES
    e_tiles = n_exp // te
    n_items = (n // tn) * e_tiles

    def item(k, lag):
        it = jnp.clip(k - lag, 0, n_items - 1)
        return it // e_tiles, it % e_tiles

    kern = functools.partial(_peer_kernel, n_items=n_items, e_tiles=e_tiles)
    return pl.pallas_call(
        kern,
        grid=(n_items + 2,),
        in_specs=[
            pl.BlockSpec((d, tn), lambda k: (0, item(k, 0)[0])),
            pl.BlockSpec((te, d), lambda k: (item(k, 0)[1], 0)),
            pl.BlockSpec((d, te), lambda k: (0, item(k, 2)[1])),
            pl.BlockSpec((heads, i1_per_tile, tn), lambda k: (0, item(k, 1)[1], item(k, 1)[0])),
            pl.BlockSpec((heads, i1_per_tile, tn), lambda k: (0, item(k, 1)[1], item(k, 1)[0])),
            pl.BlockSpec((heads, tn // LANES, N_KEYS, LANES), lambda k: (0, item(k, 1)[0], 0, 0)),
            pl.BlockSpec((heads, tn // LANES, N_KEYS, LANES), lambda k: (0, item(k, 1)[0], 0, 0)),
            pl.BlockSpec((tn, d), lambda k: (item(k, 2)[0], 0)),
            pl.BlockSpec((1, d), lambda k: (0, 0)),
        ],
        out_specs=pl.BlockSpec((tn, d), lambda k: (item(k, 2)[0], 0)),
        out_shape=jax.ShapeDtypeStruct((n, d), F32),
        scratch_shapes=[
            pltpu.VMEM((te, tn), F32),
            pltpu.VMEM((te, tn), F32),
            pltpu.VMEM((tn // LANES, te, LANES), BF16),
            pltpu.VMEM((tn // LANES, te, LANES), BF16),
            pltpu.VMEM((d, tn), F32),
        ],
        compiler_params=_cparams(("arbitrary",)),
        name="peer_dense",
    )(xnt, u, vt, g1, n1, g2, r2, x1, final_norm)


def _layer(x2, batch, seq, norm_mix, w_in, conv_w, conv_b, w_a, w_b, w_o, norm_ffn, w_q, sub_keys,
           expert_u, expert_v, out_norm):
    d = x2.shape[1]
    qkv, cbo, gates = _inproj(x2, norm_mix.reshape(1, d), w_in.astype(BF16),
                              conv_w.reshape(CONV_K, -1), conv_b.reshape(1, -1), seq)
    att = _attention(qkv, batch, seq)
    keys = sub_keys.reshape(-1, N_KEYS, sub_keys.shape[-1]).astype(BF16)
    x1, xnt, st = _mid(att, cbo, gates, x2, w_a.astype(BF16), w_b.astype(BF16), w_o.astype(BF16),
                       norm_ffn.reshape(1, d), w_q.T.astype(BF16), keys)
    g1, n1, g2, r2 = _topk(st)
    return _peer(xnt, expert_u.astype(BF16), expert_v.T.astype(BF16), g1, n1, g2, r2, x1, out_norm)


def kernel(x, norm_mix, w_in, conv_w, conv_b, w_branch_a, w_branch_b, w_out, norm_ffn, w_q, sub_keys,
           expert_u, expert_v, final_norm):
    batch, seq, d = x.shape
    depth = w_in.shape[0]
    assert depth == 1, "the final RMSNorm is fused into the last layer's PEER kernel"
    x2 = x.reshape(batch * seq, d)
    out = _layer(x2, batch, seq, norm_mix[0], w_in[0], conv_w[0], conv_b[0], w_branch_a[0],
                 w_branch_b[0], w_out[0], norm_ffn[0], w_q[0], sub_keys[0], expert_u[0], expert_v[0],
                 final_norm.reshape(1, d))
    return out.reshape(batch, seq, d)
```

```python
import functools
import math

import jax
import jax.numpy as jnp
from jax import lax
from jax.experimental import pallas as pl
from jax.experimental.pallas import tpu as pltpu

F32 = jnp.float32
BF16 = jnp.bfloat16

RMS_EPS = 1e-6
SB_HEADS = 8
SB_HEAD_DIM = 64
SB_WIDTH = SB_HEADS * SB_HEAD_DIM
CONV_K = 3
PEER_HEADS = 8
N_KEYS = 128
PEER_TOPK = 16
LANES = 128
SUBLANES = 8
BF16_ROWS = 16
VMEM_LIMIT_BYTES = 56 * 1024 * 1024
EXP_UNDERFLOW_LOG = -110.0

IN_TM = 512
ATT_TQ = 256
ATT_TK = 256
MID_TM = 512
TOPK_TN = 128
PEER_TN = 512
PEER_TE = 2048


def _cparams(sem):
    return pltpu.CompilerParams(dimension_semantics=sem, vmem_limit_bytes=VMEM_LIMIT_BYTES)


def _inproj_kernel(x_ref, nm_ref, w_ref, cw_ref, cb_ref, qkv_ref, cbo_ref, gate_ref, carry_ref,
                   *, tiles_per_seq, width):
    i = pl.program_id(0)
    x = x_ref[...]
    ms = jnp.mean(x * x, axis=-1, keepdims=True)
    h = (x * lax.rsqrt(ms + RMS_EPS) * nm_ref[...]).astype(BF16)
    o3 = 3 * width
    qkv_ref[...] = jnp.dot(h, w_ref[:, 0:o3], preferred_element_type=F32).astype(BF16)

    c = jnp.dot(h, w_ref[:, o3:2 * o3], preferred_element_type=F32)
    cb, cc, cu = c[:, 0:width], c[:, width:2 * width], c[:, 2 * width:3 * width]
    z = cc * cu
    tm = z.shape[0]

    @pl.when(i % tiles_per_seq == 0)
    def _():
        carry_ref[...] = jnp.zeros_like(carry_ref)

    prev = carry_ref[...]
    rows = lax.broadcasted_iota(jnp.int32, z.shape, 0)
    z1 = jnp.where(rows == 0, prev[7:8, :], pltpu.roll(z, 1, 0))
    z2 = jnp.where(rows == 0, prev[6:7, :], jnp.where(rows == 1, prev[7:8, :], pltpu.roll(z, 2, 0)))
    y = cw_ref[0:1, :] * z2 + cw_ref[1:2, :] * z1 + cw_ref[2:3, :] * z + cb_ref[...]
    cbo_ref[...] = (cb * y).astype(BF16)
    carry_ref[...] = z[tm - SUBLANES:, :]

    g = jnp.dot(h, w_ref[:, 2 * o3:], preferred_element_type=F32)
    gate_ref[...] = jax.nn.sigmoid(g)


def _inproj(x2, norm_mix, w_in, conv_w, conv_b, seq):
    n, d = x2.shape
    width = SB_WIDTH
    in_width = w_in.shape[1]
    tm = IN_TM
    kern = functools.partial(_inproj_kernel, tiles_per_seq=seq // tm, width=width)
    return pl.pallas_call(
        kern,
        grid=(n // tm,),
        in_specs=[
            pl.BlockSpec((tm, d), lambda i: (i, 0)),
            pl.BlockSpec((1, d), lambda i: (0, 0)),
            pl.BlockSpec((d, in_width), lambda i: (0, 0)),
            pl.BlockSpec((CONV_K, width), lambda i: (0, 0)),
            pl.BlockSpec((1, width), lambda i: (0, 0)),
        ],
        out_specs=[
            pl.BlockSpec((tm, 3 * width), lambda i: (i, 0)),
            pl.BlockSpec((tm, width), lambda i: (i, 0)),
            pl.BlockSpec((tm, 2 * d), lambda i: (i, 0)),
        ],
        out_shape=[
            jax.ShapeDtypeStruct((n, 3 * width), BF16),
            jax.ShapeDtypeStruct((n, width), BF16),
            jax.ShapeDtypeStruct((n, 2 * d), F32),
        ],
        scratch_shapes=[pltpu.VMEM((SUBLANES, width), F32)],
        compiler_params=_cparams(("arbitrary",)),
        name="inproj",
    )(x2, norm_mix, w_in, conv_w, conv_b)


def _att_block(qh, kj, vj, tri, c, o, mask):
    z = lax.dot_general(qh, kj, (((1,), (1,)), ((), ())), preferred_element_type=F32)
    sp = jnp.maximum(z, 0.0) + jnp.log(1.0 + jnp.exp(-jnp.abs(z)))
    l1mb = -sp
    if mask is not None:
        l1mb = jnp.where(mask, l1mb, 0.0)
    hi = l1mb.astype(BF16)
    lo = (l1mb - hi.astype(F32)).astype(BF16)
    excl = (jnp.dot(hi, tri, preferred_element_type=F32)
            + jnp.dot(lo, tri, preferred_element_type=F32))
    a = jnp.exp((z - sp) + excl + c)
    if mask is not None:
        a = jnp.where(mask, a, 0.0)
    o = o + jnp.dot(a.astype(BF16), vj, preferred_element_type=F32)
    c = c + jnp.sum(l1mb, axis=-1, keepdims=True)
    return c, o


def _attention_kernel(q_ref, k_ref, v_ref, o_ref, *, tq, tk, scale):
    i = pl.program_id(2)
    q = q_ref[...] * jnp.asarray(scale, BF16)
    lane = lax.broadcasted_iota(jnp.int32, q.shape, 1)
    kr = lax.broadcasted_iota(jnp.int32, (tk, tk), 0)
    kc = lax.broadcasted_iota(jnp.int32, (tk, tk), 1)
    tri = jnp.where(kr > kc, 1.0, 0.0).astype(BF16)
    qr = lax.broadcasted_iota(jnp.int32, (tq, tk), 0)
    qc = lax.broadcasted_iota(jnp.int32, (tq, tk), 1)
    diag_mask = qc < qr

    zero = jnp.zeros_like(q)
    qs = (jnp.where(lane < SB_HEAD_DIM, q, zero), jnp.where(lane >= SB_HEAD_DIM, q, zero))

    def both_heads(j, cs, os_, mask):
        s = pl.multiple_of(j * tk, tk)
        kj = k_ref[pl.ds(s, tk), :]
        vj = v_ref[pl.ds(s, tk), :]
        new = [_att_block(qs[h], kj, vj, tri, cs[h], os_[h], mask) for h in range(2)]
        return (new[0][0], new[1][0]), (new[0][1], new[1][1])

    def alive(cs):
        return (jnp.max(jnp.maximum(cs[0], cs[1])) >= EXP_UNDERFLOW_LOG).astype(jnp.int32)

    def write(os_):
        o_ref[...] = jnp.where(lane < SB_HEAD_DIM, os_[0], os_[1]).astype(o_ref.dtype)

    c0 = jnp.zeros((tq, 1), F32)
    o0 = jnp.zeros((tq, LANES), F32)

    @pl.when(i == 0)
    def _():
        _, os_ = both_heads(i, (c0, c0), (o0, o0), diag_mask)
        write(os_)

    @pl.when(i > 0)
    def _():
        cs, os_ = both_heads(i, (c0, c0), (o0, o0), diag_mask)
        cs, os_ = both_heads(i - 1, cs, os_, None)

        def cond(state):
            jj, _, _, live = state
            return jnp.logical_and(jj <= i, live > 0)

        def body(state):
            jj, cs, os_, _ = state
            cs, os_ = both_heads(i - jj, cs, os_, None)
            return jj + 1, cs, os_, alive(cs)

        _, _, os_, _ = lax.while_loop(cond, body, (jnp.int32(2), cs, os_, alive(cs)))
        write(os_)


def _attention(qkv, batch, seq):
    n = qkv.shape[0]
    tq, tk = ATT_TQ, ATT_TK
    assert tq == tk
    pairs = SB_WIDTH // LANES
    nq = seq // tq
    kern = functools.partial(_attention_kernel, tq=tq, tk=tk, scale=SB_HEAD_DIM ** -0.5)
    return pl.pallas_call(
        kern,
        grid=(batch, pairs, nq),
        in_specs=[
            pl.BlockSpec((tq, LANES), lambda b, p, i: (b * nq + i, p)),
            pl.BlockSpec((seq, LANES), lambda b, p, i: (b, pairs + p)),
            pl.BlockSpec((seq, LANES), lambda b, p, i: (b, 2 * pairs + p)),
        ],
        out_specs=pl.BlockSpec((tq, LANES), lambda b, p, i: (b * nq + i, p)),
        out_shape=jax.ShapeDtypeStruct((n, SB_WIDTH), BF16),
        compiler_params=_cparams(("arbitrary", "arbitrary", "arbitrary")),
        name="stickbreak_attention",
    )(qkv, qkv, qkv)


def _mid_kernel(att_ref, cbo_ref, gate_ref, x_ref, wa_ref, wb_ref, wo_ref, nf_ref, wqt_ref, keys_ref,
                x1_ref, xnt_ref, st_ref):
    d = x_ref.shape[1]
    ya = jnp.dot(att_ref[...], wa_ref[...], preferred_element_type=F32)
    yb = jnp.dot(cbo_ref[...], wb_ref[...], preferred_element_type=F32)
    merged = (gate_ref[:, 0:d] * ya + gate_ref[:, d:2 * d] * yb).astype(BF16)
    x1 = x_ref[...] + jnp.dot(merged, wo_ref[...], preferred_element_type=F32)
    x1_ref[...] = x1
    ms = jnp.mean(x1 * x1, axis=-1, keepdims=True)
    xn = x1 * lax.rsqrt(ms + RMS_EPS) * nf_ref[...]
    xnt_ref[...] = xn.T.astype(BF16)
    qt = lax.dot_general(wqt_ref[...], xn.astype(BF16), (((1,), (1,)), ((), ())),
                         preferred_element_type=F32)
    for hp in range(st_ref.shape[0]):
        qhp = qt[hp * N_KEYS:(hp + 1) * N_KEYS, :].astype(BF16)
        st_ref[hp] = jnp.dot(keys_ref[hp], qhp, preferred_element_type=F32)


def _mid(att, cbo, gates, x2, wa, wb, wo, norm_ffn, wqt, keys):
    n, d = x2.shape
    tm = MID_TM
    nhp = keys.shape[0]
    const2 = lambda i: (0, 0)
    return pl.pallas_call(
        _mid_kernel,
        grid=(n // tm,),
        in_specs=[
            pl.BlockSpec((tm, SB_WIDTH), lambda i: (i, 0)),
            pl.BlockSpec((tm, SB_WIDTH), lambda i: (i, 0)),
            pl.BlockSpec((tm, 2 * d), lambda i: (i, 0)),
            pl.BlockSpec((tm, d), lambda i: (i, 0)),
            pl.BlockSpec(wa.shape, const2),
            pl.BlockSpec(wb.shape, const2),
            pl.BlockSpec(wo.shape, const2),
            pl.BlockSpec((1, d), const2),
            pl.BlockSpec(wqt.shape, const2),
            pl.BlockSpec(keys.shape, lambda i: (0, 0, 0)),
        ],
        out_specs=[
            pl.BlockSpec((tm, d), lambda i: (i, 0)),
            pl.BlockSpec((d, tm), lambda i: (0, i)),
            pl.BlockSpec((nhp, N_KEYS, tm), lambda i: (0, 0, i)),
        ],
        out_shape=[
            jax.ShapeDtypeStruct((n, d), F32),
            jax.ShapeDtypeStruct((d, n), BF16),
            jax.ShapeDtypeStruct((nhp, N_KEYS, n), F32),
        ],
        compiler_params=_cparams(("arbitrary",)),
        name="mid_proj",
    )(att, cbo, gates, x2, wa, wb, wo, norm_ffn, wqt, keys)


def _candidate_cells():
    return [(a, b) for a in range(PEER_TOPK) for b in range(PEER_TOPK) if (a + 1) * (b + 1) <= PEER_TOPK]


def _topk_kernel(st_ref, g1_ref, n1_ref, g2_ref, r2_ref, rank_scr, tops_scr):
    k = PEER_TOPK
    neg_inf = jnp.asarray(-jnp.inf, F32)
    kio = lax.broadcasted_iota(jnp.int32, (N_KEYS, LANES), 0).astype(F32)
    sub = lax.broadcasted_iota(jnp.int32, (SUBLANES, LANES), 0)

    def record_top(p, h, r, m):
        slot = p * k + r
        tops_scr[slot] = jnp.where(sub == h, m, tops_scr[slot])

    def extract_all_equal(h, tied):
        for p in range(2):
            x = st_ref[2 * h + p]
            rank = jnp.full((N_KEYS, LANES), float(k), F32)
            for r in range(k):
                m = jnp.max(x, axis=0, keepdims=True)
                hit = x == m
                rank = jnp.where(hit, float(r), rank)
                x = jnp.where(hit, neg_inf, x)
                record_top(p, h, r, m)
            rank_scr[2 * h + p] = rank
            removed = jnp.sum(jnp.where(rank < float(k), 1.0, 0.0), axis=0, keepdims=True)
            tied = jnp.maximum(tied, jnp.where(removed != float(k), 1.0, 0.0))
        return tied

    tied = lax.fori_loop(0, PEER_HEADS, extract_all_equal, jnp.zeros((1, LANES), F32))

    @pl.when(jnp.max(tied) > 0.0)
    def _():
        def extract_first(hp, _):
            x = st_ref[hp]
            rank = jnp.full((N_KEYS, LANES), float(k), F32)
            h = hp // 2
            p = hp % 2
            for r in range(k):
                m = jnp.max(x, axis=0, keepdims=True)
                first = jnp.min(jnp.where(x == m, kio, float(N_KEYS)), axis=0, keepdims=True)
                hit = kio == first
                rank = jnp.where(hit, float(r), rank)
                x = jnp.where(hit, neg_inf, x)
                record_top(p, h, r, m)
            rank_scr[hp] = rank
            return 0

        lax.fori_loop(0, st_ref.shape[0], extract_first, 0)

    t1 = [tops_scr[a] for a in range(k)]
    t2 = [tops_scr[k + b] for b in range(k)]
    cells = _candidate_cells()
    val = {ab: t1[ab[0]] + t2[ab[1]] for ab in cells}
    beat = {ab: jnp.full((SUBLANES, LANES), float((ab[0] + 1) * (ab[1] + 1) - 1), F32) for ab in cells}
    for ip, p_ in enumerate(cells):
        for q_ in cells[ip + 1:]:
            if p_[0] <= q_[0] and p_[1] <= q_[1]:
                continue
            t = jnp.where(val[p_] >= val[q_], 1.0, 0.0)
            beat[q_] = beat[q_] + t
            beat[p_] = beat[p_] + (1.0 - t)
    top = val[(0, 0)]
    zsum = jnp.zeros((SUBLANES, LANES), F32)
    n_sel = [jnp.zeros((SUBLANES, LANES), F32) for _ in range(k)]
    for ab in cells:
        sel = beat[ab] < float(k)
        zsum = zsum + jnp.where(sel, jnp.exp(val[ab] - top), 0.0)
        n_sel[ab[0]] = n_sel[ab[0]] + jnp.where(sel, 1.0, 0.0)
    zinv = 1.0 / zsum

    for h in range(PEER_HEADS):
        rank1 = rank_scr[2 * h]
        rank2 = rank_scr[2 * h + 1]
        m1 = t1[0][h:h + 1, :]
        m2 = t2[0][h:h + 1, :]
        g1_ref[h] = jnp.where(rank1 < float(k), jnp.exp(st_ref[2 * h] - m1), 0.0)
        g2 = jnp.where(rank2 < float(k), jnp.exp(st_ref[2 * h + 1] - m2) * zinv[h:h + 1, :], 0.0)
        g2_ref[h] = g2.astype(BF16)
        r2_ref[h] = rank2.astype(BF16)
        n1 = jnp.zeros((N_KEYS, LANES), F32)
        for a in range(k):
            n1 = jnp.where(rank1 == float(a), n_sel[a][h:h + 1, :], n1)
        n1_ref[h] = n1


def _topk(st):
    nhp, nk, n = st.shape
    tn = TOPK_TN
    heads = nhp // 2
    spec = pl.BlockSpec((heads, nk, tn), lambda i: (0, 0, i))
    return pl.pallas_call(
        _topk_kernel,
        grid=(n // tn,),
        in_specs=[pl.BlockSpec((nhp, nk, tn), lambda i: (0, 0, i))],
        out_specs=[spec, spec, spec, spec],
        out_shape=[
            jax.ShapeDtypeStruct((heads, nk, n), F32),
            jax.ShapeDtypeStruct((heads, nk, n), F32),
            jax.ShapeDtypeStruct((heads, nk, n), BF16),
            jax.ShapeDtypeStruct((heads, nk, n), BF16),
        ],
        scratch_shapes=[
            pltpu.VMEM((nhp, nk, tn), F32),
            pltpu.VMEM((2 * PEER_TOPK, SUBLANES, tn), F32),
        ],
        compiler_params=_cparams(("arbitrary",)),
        name="peer_topk",
    )(st)


def _peer_kernel(xnt_ref, u_ref, vt_ref, g1_ref, n1_ref, g2_ref, r2_ref, x1_ref, fn_ref, out_ref,
                 s_scr, p_scr, acc_scr):
    e = pl.program_id(1)
    te = u_ref.shape[0]

    @pl.when(e == 0)
    def _():
        acc_scr[...] = jnp.zeros_like(acc_scr)

    s_scr[...] = jnp.dot(u_ref[...], xnt_ref[...], preferred_element_type=F32)

    def i1_body(j, _):
        rows = pl.ds(pl.multiple_of(j * N_KEYS, N_KEYS), N_KEYS)
        s = s_scr[rows, :]
        act = 0.5 * s * (1.0 + lax.erf(s * math.sqrt(0.5)))
        tn = s.shape[1]
        group = (BF16_ROWS, tn)
        w = jnp.zeros((N_KEYS // BF16_ROWS,) + group, BF16)
        for h in range(PEER_HEADS):
            g1 = jnp.broadcast_to(g1_ref[h, pl.ds(j, 1), :], group).astype(BF16)
            n1 = jnp.broadcast_to(n1_ref[h, pl.ds(j, 1), :], group).astype(BF16)
            w = w + jnp.where(r2_ref[h] < n1, g2_ref[h] * g1, jnp.zeros_like(w))
        p_scr[rows, :] = act.astype(BF16) * w.reshape(N_KEYS, tn)
        return 0

    lax.fori_loop(0, te // N_KEYS, i1_body, 0)
    acc_scr[...] += jnp.dot(vt_ref[...], p_scr[...], preferred_element_type=F32)

    @pl.when(e == pl.num_programs(1) - 1)
    def _():
        y = x1_ref[...] + acc_scr[...].T
        ms = jnp.mean(y * y, axis=-1, keepdims=True)
        out_ref[...] = y * lax.rsqrt(ms + RMS_EPS) * fn_ref[...]


def _peer(xnt, u, vt, g1, n1, g2, r2, x1, final_norm):
    d, n = xnt.shape
    n_exp = u.shape[0]
    tn, te = PEER_TN, PEER_TE
    heads = g1.shape[0]
    i1_per_tile = te // N_KEYS
    assert i1_per_tile % SUBLANES == 0
    row_groups = N_KEYS // BF16_ROWS
    g2 = g2.reshape(heads, row_groups, BF16_ROWS, n)
    r2 = r2.reshape(heads, row_groups, BF16_ROWS, n)
    return pl.pallas_call(
        _peer_kernel,
        grid=(n // tn, n_exp // te),
        in_specs=[
            pl.BlockSpec((d, tn), lambda t, e: (0, t)),
            pl.BlockSpec((te, d), lambda t, e: (e, 0)),
            pl.BlockSpec((d, te), lambda t, e: (0, e)),
            pl.BlockSpec((heads, i1_per_tile, tn), lambda t, e: (0, e, t)),
            pl.BlockSpec((heads, i1_per_tile, tn), lambda t, e: (0, e, t)),
            pl.BlockSpec((heads, row_groups, BF16_ROWS, tn), lambda t, e: (0, 0, 0, t)),
            pl.BlockSpec((heads, row_groups, BF16_ROWS, tn), lambda t, e: (0, 0, 0, t)),
            pl.BlockSpec((tn, d), lambda t, e: (t, 0)),
            pl.BlockSpec((1, d), lambda t, e: (0, 0)),
        ],
        out_specs=pl.BlockSpec((tn, d), lambda t, e: (t, 0)),
        out_shape=jax.ShapeDtypeStruct((n, d), F32),
        scratch_shapes=[
            pltpu.VMEM((te, tn), F32),
            pltpu.VMEM((te, tn), BF16),
            pltpu.VMEM((d, tn), F32),
        ],
        compiler_params=_cparams(("arbitrary", "arbitrary")),
        name="peer_dense",
    )(xnt, u, vt, g1, n1, g2, r2, x1, final_norm)


def _layer(x2, batch, seq, norm_mix, w_in, conv_w, conv_b, w_a, w_b, w_o, norm_ffn, w_q, sub_keys,
           expert_u, expert_v, out_norm):
    d = x2.shape[1]
    qkv, cbo, gates = _inproj(x2, norm_mix.reshape(1, d), w_in.astype(BF16),
                              conv_w.reshape(CONV_K, -1), conv_b.reshape(1, -1), seq)
    att = _attention(qkv, batch, seq)
    keys = sub_keys.reshape(-1, N_KEYS, sub_keys.shape[-1]).astype(BF16)
    x1, xnt, st = _mid(att, cbo, gates, x2, w_a.astype(BF16), w_b.astype(BF16), w_o.astype(BF16),
                       norm_ffn.reshape(1, d), w_q.T.astype(BF16), keys)
    g1, n1, g2, r2 = _topk(st)
    return _peer(xnt, expert_u.astype(BF16), expert_v.T.astype(BF16), g1, n1, g2, r2, x1, out_norm)


def kernel(x, norm_mix, w_in, conv_w, conv_b, w_branch_a, w_branch_b, w_out, norm_ffn, w_q, sub_keys,
           expert_u, expert_v, final_norm):
    batch, seq, d = x.shape
    depth = w_in.shape[0]
    assert depth == 1, "the final RMSNorm is fused into the last layer's PEER kernel"
    x2 = x.reshape(batch * seq, d)
    out = _layer(x2, batch, seq, norm_mix[0], w_in[0], conv_w[0], conv_b[0], w_branch_a[0],
                 w_branch_b[0], w_out[0], norm_ffn[0], w_q[0], sub_keys[0], expert_u[0], expert_v[0],
                 final_norm.reshape(1, d))
    return out.reshape(batch, seq, d)
```

```python
import functools
import math

import jax
import jax.numpy as jnp
from jax import lax
from jax.experimental import pallas as pl
from jax.experimental.pallas import tpu as pltpu

F32 = jnp.float32
BF16 = jnp.bfloat16

RMS_EPS = 1e-6
SB_HEADS = 8
SB_HEAD_DIM = 64
SB_WIDTH = SB_HEADS * SB_HEAD_DIM
CONV_K = 3
PEER_HEADS = 8
N_KEYS = 128
PEER_TOPK = 16
LANES = 128
SUBLANES = 8
BF16_ROWS = 16
VMEM_LIMIT_BYTES = 56 * 1024 * 1024
EXP_UNDERFLOW_LOG = -110.0

IN_TM = 512
ATT_TQ = 256
ATT_TK = 256
MID_TM = 512
TOPK_TN = 128
PEER_TN = 512
PEER_TE = 2048
PEER_LANE_CHUNK = 256


def _cparams(sem):
    return pltpu.CompilerParams(dimension_semantics=sem, vmem_limit_bytes=VMEM_LIMIT_BYTES)


def _inproj_kernel(x_ref, nm_ref, w_ref, cw_ref, cb_ref, qkv_ref, cbo_ref, gate_ref, carry_ref,
                   *, tiles_per_seq, width):
    i = pl.program_id(0)
    x = x_ref[...]
    ms = jnp.mean(x * x, axis=-1, keepdims=True)
    h = (x * lax.rsqrt(ms + RMS_EPS) * nm_ref[...]).astype(BF16)
    o3 = 3 * width
    qkv_ref[...] = jnp.dot(h, w_ref[:, 0:o3], preferred_element_type=F32).astype(BF16)

    c = jnp.dot(h, w_ref[:, o3:2 * o3], preferred_element_type=F32)
    cb, cc, cu = c[:, 0:width], c[:, width:2 * width], c[:, 2 * width:3 * width]
    z = cc * cu
    tm = z.shape[0]

    @pl.when(i % tiles_per_seq == 0)
    def _():
        carry_ref[...] = jnp.zeros_like(carry_ref)

    prev = carry_ref[...]
    rows = lax.broadcasted_iota(jnp.int32, z.shape, 0)
    z1 = jnp.where(rows == 0, prev[7:8, :], pltpu.roll(z, 1, 0))
    z2 = jnp.where(rows == 0, prev[6:7, :], jnp.where(rows == 1, prev[7:8, :], pltpu.roll(z, 2, 0)))
    y = cw_ref[0:1, :] * z2 + cw_ref[1:2, :] * z1 + cw_ref[2:3, :] * z + cb_ref[...]
    cbo_ref[...] = (cb * y).astype(BF16)
    carry_ref[...] = z[tm - SUBLANES:, :]

    g = jnp.dot(h, w_ref[:, 2 * o3:], preferred_element_type=F32)
    gate_ref[...] = 0.5 * jnp.tanh(0.5 * g) + 0.5


def _inproj(x2, norm_mix, w_in, conv_w, conv_b, seq):
    n, d = x2.shape
    width = SB_WIDTH
    in_width = w_in.shape[1]
    tm = IN_TM
    kern = functools.partial(_inproj_kernel, tiles_per_seq=seq // tm, width=width)
    return pl.pallas_call(
        kern,
        grid=(n // tm,),
        in_specs=[
            pl.BlockSpec((tm, d), lambda i: (i, 0)),
            pl.BlockSpec((1, d), lambda i: (0, 0)),
            pl.BlockSpec((d, in_width), lambda i: (0, 0)),
            pl.BlockSpec((CONV_K, width), lambda i: (0, 0)),
            pl.BlockSpec((1, width), lambda i: (0, 0)),
        ],
        out_specs=[
            pl.BlockSpec((tm, 3 * width), lambda i: (i, 0)),
            pl.BlockSpec((tm, width), lambda i: (i, 0)),
            pl.BlockSpec((tm, 2 * d), lambda i: (i, 0)),
        ],
        out_shape=[
            jax.ShapeDtypeStruct((n, 3 * width), BF16),
            jax.ShapeDtypeStruct((n, width), BF16),
            jax.ShapeDtypeStruct((n, 2 * d), F32),
        ],
        scratch_shapes=[pltpu.VMEM((SUBLANES, width), F32)],
        compiler_params=_cparams(("arbitrary",)),
        name="inproj",
    )(x2, norm_mix, w_in, conv_w, conv_b)


def _att_block(qh, kj, vj, tri, c, o, mask):
    z = lax.dot_general(qh, kj, (((1,), (1,)), ((), ())), preferred_element_type=F32)
    sp = jnp.maximum(z, 0.0) + jnp.log(1.0 + jnp.exp(-jnp.abs(z)))
    l1mb = -sp
    if mask is not None:
        l1mb = jnp.where(mask, l1mb, 0.0)
    hi = l1mb.astype(BF16)
    lo = (l1mb - hi.astype(F32)).astype(BF16)
    excl = (jnp.dot(hi, tri, preferred_element_type=F32)
            + jnp.dot(lo, tri, preferred_element_type=F32))
    a = jnp.exp((z - sp) + excl + c)
    if mask is not None:
        a = jnp.where(mask, a, 0.0)
    o = o + jnp.dot(a.astype(BF16), vj, preferred_element_type=F32)
    c = c + jnp.sum(l1mb, axis=-1, keepdims=True)
    return c, o


def _attention_kernel(q_ref, k_ref, v_ref, o_ref, *, tq, tk, scale):
    i = pl.program_id(2)
    q = q_ref[...] * jnp.asarray(scale, BF16)
    lane = lax.broadcasted_iota(jnp.int32, q.shape, 1)
    kr = lax.broadcasted_iota(jnp.int32, (tk, tk), 0)
    kc = lax.broadcasted_iota(jnp.int32, (tk, tk), 1)
    tri = jnp.where(kr > kc, 1.0, 0.0).astype(BF16)
    qr = lax.broadcasted_iota(jnp.int32, (tq, tk), 0)
    qc = lax.broadcasted_iota(jnp.int32, (tq, tk), 1)
    diag_mask = qc < qr

    zero = jnp.zeros_like(q)
    qs = (jnp.where(lane < SB_HEAD_DIM, q, zero), jnp.where(lane >= SB_HEAD_DIM, q, zero))

    def both_heads(j, cs, os_, mask):
        s = pl.multiple_of(j * tk, tk)
        kj = k_ref[pl.ds(s, tk), :]
        vj = v_ref[pl.ds(s, tk), :]
        new = [_att_block(qs[h], kj, vj, tri, cs[h], os_[h], mask) for h in range(2)]
        return (new[0][0], new[1][0]), (new[0][1], new[1][1])

    def alive(cs):
        return (jnp.max(jnp.maximum(cs[0], cs[1])) >= EXP_UNDERFLOW_LOG).astype(jnp.int32)

    def write(os_):
        o_ref[...] = jnp.where(lane < SB_HEAD_DIM, os_[0], os_[1]).astype(o_ref.dtype)

    c0 = jnp.zeros((tq, 1), F32)
    o0 = jnp.zeros((tq, LANES), F32)

    @pl.when(i == 0)
    def _():
        _, os_ = both_heads(i, (c0, c0), (o0, o0), diag_mask)
        write(os_)

    @pl.when(i > 0)
    def _():
        cs, os_ = both_heads(i, (c0, c0), (o0, o0), diag_mask)
        cs, os_ = both_heads(i - 1, cs, os_, None)

        def cond(state):
            jj, _, _, live = state
            return jnp.logical_and(jj <= i, live > 0)

        def body(state):
            jj, cs, os_, _ = state
            cs, os_ = both_heads(i - jj, cs, os_, None)
            return jj + 1, cs, os_, alive(cs)

        _, _, os_, _ = lax.while_loop(cond, body, (jnp.int32(2), cs, os_, alive(cs)))
        write(os_)


def _attention(qkv, batch, seq):
    n = qkv.shape[0]
    tq, tk = ATT_TQ, ATT_TK
    assert tq == tk
    pairs = SB_WIDTH // LANES
    nq = seq // tq
    kern = functools.partial(_attention_kernel, tq=tq, tk=tk, scale=SB_HEAD_DIM ** -0.5)
    return pl.pallas_call(
        kern,
        grid=(batch, pairs, nq),
        in_specs=[
            pl.BlockSpec((tq, LANES), lambda b, p, i: (b * nq + i, p)),
            pl.BlockSpec((seq, LANES), lambda b, p, i: (b, pairs + p)),
            pl.BlockSpec((seq, LANES), lambda b, p, i: (b, 2 * pairs + p)),
        ],
        out_specs=pl.BlockSpec((tq, LANES), lambda b, p, i: (b * nq + i, p)),
        out_shape=jax.ShapeDtypeStruct((n, SB_WIDTH), BF16),
        compiler_params=_cparams(("arbitrary", "arbitrary", "arbitrary")),
        name="stickbreak_attention",
    )(qkv, qkv, qkv)


def _mid_kernel(att_ref, cbo_ref, gate_ref, x_ref, wa_ref, wb_ref, wo_ref, nf_ref, wqt_ref, keys_ref,
                x1_ref, xnt_ref, st_ref):
    d = x_ref.shape[1]
    ya = jnp.dot(att_ref[...], wa_ref[...], preferred_element_type=F32)
    yb = jnp.dot(cbo_ref[...], wb_ref[...], preferred_element_type=F32)
    merged = (gate_ref[:, 0:d] * ya + gate_ref[:, d:2 * d] * yb).astype(BF16)
    x1 = x_ref[...] + jnp.dot(merged, wo_ref[...], preferred_element_type=F32)
    x1_ref[...] = x1
    ms = jnp.mean(x1 * x1, axis=-1, keepdims=True)
    xn = x1 * lax.rsqrt(ms + RMS_EPS) * nf_ref[...]
    xnt_ref[...] = xn.T.astype(BF16)
    qt = lax.dot_general(wqt_ref[...], xn.astype(BF16), (((1,), (1,)), ((), ())),
                         preferred_element_type=F32)
    for hp in range(st_ref.shape[0]):
        qhp = qt[hp * N_KEYS:(hp + 1) * N_KEYS, :].astype(BF16)
        st_ref[hp] = jnp.dot(keys_ref[hp], qhp, preferred_element_type=F32)


def _mid(att, cbo, gates, x2, wa, wb, wo, norm_ffn, wqt, keys):
    n, d = x2.shape
    tm = MID_TM
    nhp = keys.shape[0]
    const2 = lambda i: (0, 0)
    return pl.pallas_call(
        _mid_kernel,
        grid=(n // tm,),
        in_specs=[
            pl.BlockSpec((tm, SB_WIDTH), lambda i: (i, 0)),
            pl.BlockSpec((tm, SB_WIDTH), lambda i: (i, 0)),
            pl.BlockSpec((tm, 2 * d), lambda i: (i, 0)),
            pl.BlockSpec((tm, d), lambda i: (i, 0)),
            pl.BlockSpec(wa.shape, const2),
            pl.BlockSpec(wb.shape, const2),
            pl.BlockSpec(wo.shape, const2),
            pl.BlockSpec((1, d), const2),
            pl.BlockSpec(wqt.shape, const2),
            pl.BlockSpec(keys.shape, lambda i: (0, 0, 0)),
        ],
        out_specs=[
            pl.BlockSpec((tm, d), lambda i: (i, 0)),
            pl.BlockSpec((d, tm), lambda i: (0, i)),
            pl.BlockSpec((nhp, N_KEYS, tm), lambda i: (0, 0, i)),
        ],
        out_shape=[
            jax.ShapeDtypeStruct((n, d), F32),
            jax.ShapeDtypeStruct((d, n), BF16),
            jax.ShapeDtypeStruct((nhp, N_KEYS, n), F32),
        ],
        compiler_params=_cparams(("arbitrary",)),
        name="mid_proj",
    )(att, cbo, gates, x2, wa, wb, wo, norm_ffn, wqt, keys)


def _candidate_cells():
    return [(a, b) for a in range(PEER_TOPK) for b in range(PEER_TOPK) if (a + 1) * (b + 1) <= PEER_TOPK]


def _topk_kernel(st_ref, g1_ref, n1_ref, g2_ref, r2_ref, rank_scr, tops_scr):
    k = PEER_TOPK
    neg_inf = jnp.asarray(-jnp.inf, F32)
    kio = lax.broadcasted_iota(jnp.int32, (N_KEYS, LANES), 0).astype(F32)
    sub = lax.broadcasted_iota(jnp.int32, (SUBLANES, LANES), 0)

    def record_top(p, h, r, m):
        slot = p * k + r
        tops_scr[slot] = jnp.where(sub == h, m, tops_scr[slot])

    def extract_all_equal(h2, tied):
        for hp in range(4):
            h = 2 * h2 + hp // 2
            p = hp % 2
            x = st_ref[2 * h + p]
            peeled = []
            for r in range(k):
                m = jnp.max(x, axis=0, keepdims=True)
                x = jnp.where(x == m, neg_inf, x)
                record_top(p, h, r, m)
                peeled.append(m)
            x = st_ref[2 * h + p]
            rank = jnp.full((N_KEYS, LANES), float(k), F32)
            for r in range(k):
                rank = jnp.where(x == peeled[r], float(r), rank)
            rank_scr[2 * h + p] = rank
            removed = jnp.sum(jnp.where(rank < float(k), 1.0, 0.0), axis=0, keepdims=True)
            tied = jnp.maximum(tied, jnp.where(removed != float(k), 1.0, 0.0))
        return tied

    tied = lax.fori_loop(0, PEER_HEADS // 2, extract_all_equal, jnp.zeros((1, LANES), F32))

    @pl.when(jnp.max(tied) > 0.0)
    def _():
        def extract_first(hp, _):
            x = st_ref[hp]
            rank = jnp.full((N_KEYS, LANES), float(k), F32)
            h = hp // 2
            p = hp % 2
            for r in range(k):
                m = jnp.max(x, axis=0, keepdims=True)
                first = jnp.min(jnp.where(x == m, kio, float(N_KEYS)), axis=0, keepdims=True)
                hit = kio == first
                rank = jnp.where(hit, float(r), rank)
                x = jnp.where(hit, neg_inf, x)
                record_top(p, h, r, m)
            rank_scr[hp] = rank
            return 0

        lax.fori_loop(0, st_ref.shape[0], extract_first, 0)

    t1 = [tops_scr[a] for a in range(k)]
    t2 = [tops_scr[k + b] for b in range(k)]
    cells = _candidate_cells()
    val = {ab: t1[ab[0]] + t2[ab[1]] for ab in cells}
    beat = {ab: jnp.full((SUBLANES, LANES), float((ab[0] + 1) * (ab[1] + 1) - 1), F32) for ab in cells}
    for ip, p_ in enumerate(cells):
        for q_ in cells[ip + 1:]:
            if p_[0] <= q_[0] and p_[1] <= q_[1]:
                continue
            t = jnp.where(val[p_] >= val[q_], 1.0, 0.0)
            beat[q_] = beat[q_] + t
            beat[p_] = beat[p_] + (1.0 - t)
    top = val[(0, 0)]
    zsum = jnp.zeros((SUBLANES, LANES), F32)
    n_sel = [jnp.zeros((SUBLANES, LANES), F32) for _ in range(k)]
    for ab in cells:
        sel = beat[ab] < float(k)
        zsum = zsum + jnp.where(sel, jnp.exp(val[ab] - top), 0.0)
        n_sel[ab[0]] = n_sel[ab[0]] + jnp.where(sel, 1.0, 0.0)
    zinv = 0.5 / zsum

    for h in range(PEER_HEADS):
        rank1 = rank_scr[2 * h]
        rank2 = rank_scr[2 * h + 1]
        m1 = t1[0][h:h + 1, :]
        m2 = t2[0][h:h + 1, :]
        g1_ref[h] = jnp.where(rank1 < float(k), jnp.exp(st_ref[2 * h] - m1), 0.0)
        g2 = jnp.where(rank2 < float(k), jnp.exp(st_ref[2 * h + 1] - m2) * zinv[h:h + 1, :], 0.0)
        g2_ref[h] = g2.astype(BF16)
        r2_ref[h] = rank2.astype(BF16)
        n1 = jnp.zeros((N_KEYS, LANES), F32)
        for a in range(k):
            n1 = jnp.where(rank1 == float(a), n_sel[a][h:h + 1, :], n1)
        n1_ref[h] = n1


def _topk(st):
    nhp, nk, n = st.shape
    tn = TOPK_TN
    heads = nhp // 2
    spec = pl.BlockSpec((heads, nk, tn), lambda i: (0, 0, i))
    return pl.pallas_call(
        _topk_kernel,
        grid=(n // tn,),
        in_specs=[pl.BlockSpec((nhp, nk, tn), lambda i: (0, 0, i))],
        out_specs=[spec, spec, spec, spec],
        out_shape=[
            jax.ShapeDtypeStruct((heads, nk, n), F32),
            jax.ShapeDtypeStruct((heads, nk, n), F32),
            jax.ShapeDtypeStruct((heads, nk, n), BF16),
            jax.ShapeDtypeStruct((heads, nk, n), BF16),
        ],
        scratch_shapes=[
            pltpu.VMEM((nhp, nk, tn), F32),
            pltpu.VMEM((2 * PEER_TOPK, SUBLANES, tn), F32),
        ],
        compiler_params=_cparams(("arbitrary",)),
        name="peer_topk",
    )(st)


def _peer_tick(xnt_ref, u_ref, vt_ref, g1_ref, n1_ref, g2_ref, r2_ref, acc_scr, p_scr,
               s_write, s_read, first_of_token_tile):
    te, tn = s_read.shape
    s_write[...] = jnp.dot(u_ref[...], xnt_ref[...], preferred_element_type=F32)

    group = (BF16_ROWS, PEER_LANE_CHUNK)
    for j in range(te // N_KEYS):
        rows = slice(j * N_KEYS, (j + 1) * N_KEYS)
        for lc in range(tn // PEER_LANE_CHUNK):
            cols = slice(lc * PEER_LANE_CHUNK, (lc + 1) * PEER_LANE_CHUNK)
            w = jnp.zeros((N_KEYS // BF16_ROWS,) + group, BF16)
            for h in range(PEER_HEADS):
                g1 = jnp.broadcast_to(g1_ref[h, j:j + 1, cols], group).astype(BF16)
                n1 = jnp.broadcast_to(n1_ref[h, j:j + 1, cols], group).astype(BF16)
                w = w + jnp.where(r2_ref[h, :, :, cols] < n1, g2_ref[h, :, :, cols] * g1, jnp.zeros_like(w))
            s = s_read[rows, cols]
            act = s * (1.0 + lax.erf(s * math.sqrt(0.5)))
            p_scr[rows, cols] = act.astype(BF16) * w.reshape(N_KEYS, PEER_LANE_CHUNK)

    upd = jnp.dot(vt_ref[...], p_scr[...], preferred_element_type=F32)
    acc_scr[...] = jnp.where(first_of_token_tile, upd, acc_scr[...] + upd)


def _peer_kernel(xnt_ref, u_ref, vt_ref, g1_ref, n1_ref, g2_ref, r2_ref, x1_ref, fn_ref, out_ref,
                 s0, s1, p_scr, acc_scr, *, n_items, e_tiles):
    k = pl.program_id(0)
    e_b = jnp.clip(k - 1, 0, n_items - 1) % e_tiles

    @pl.when(k == 0)
    def _():
        s1[...] = jnp.zeros_like(s1)
        acc_scr[...] = jnp.zeros_like(acc_scr)

    refs = (xnt_ref, u_ref, vt_ref, g1_ref, n1_ref, g2_ref, r2_ref, acc_scr, p_scr)

    @pl.when(k % 2 == 0)
    def _():
        _peer_tick(*refs, s0, s1, e_b == 0)

    @pl.when(k % 2 == 1)
    def _():
        _peer_tick(*refs, s1, s0, e_b == 0)

    @pl.when(jnp.logical_and(k >= 1, e_b == e_tiles - 1))
    def _():
        y = x1_ref[...] + acc_scr[...].T
        ms = jnp.mean(y * y, axis=-1, keepdims=True)
        out_ref[...] = y * lax.rsqrt(ms + RMS_EPS) * fn_ref[...]


def _peer(xnt, u, vt, g1, n1, g2, r2, x1, final_norm):
    d, n = xnt.shape
    n_exp = u.shape[0]
    tn, te = PEER_TN, PEER_TE
    heads = g1.shape[0]
    i1_per_tile = te // N_KEYS
    assert i1_per_tile % SUBLANES == 0
    row_groups = N_KEYS // BF16_ROWS
    g2 = g2.reshape(heads, row_groups, BF16_ROWS, n)
    r2 = r2.reshape(heads, row_groups, BF16_ROWS, n)
    e_tiles = n_exp // te
    n_items = (n // tn) * e_tiles

    def item(k, lag):
        it = jnp.clip(k - lag, 0, n_items - 1)
        return it // e_tiles, it % e_tiles

    kern = functools.partial(_peer_kernel, n_items=n_items, e_tiles=e_tiles)
    return pl.pallas_call(
        kern,
        grid=(n_items + 1,),
        in_specs=[
            pl.BlockSpec((d, tn), lambda k: (0, item(k, 0)[0])),
            pl.BlockSpec((te, d), lambda k: (item(k, 0)[1], 0)),
            pl.BlockSpec((d, te), lambda k: (0, item(k, 1)[1])),
            pl.BlockSpec((heads, i1_per_tile, tn), lambda k: (0, item(k, 1)[1], item(k, 1)[0])),
            pl.BlockSpec((heads, i1_per_tile, tn), lambda k: (0, item(k, 1)[1], item(k, 1)[0])),
            pl.BlockSpec((heads, row_groups, BF16_ROWS, tn), lambda k: (0, 0, 0, item(k, 1)[0])),
            pl.BlockSpec((heads, row_groups, BF16_ROWS, tn), lambda k: (0, 0, 0, item(k, 1)[0])),
            pl.BlockSpec((tn, d), lambda k: (item(k, 1)[0], 0)),
            pl.BlockSpec((1, d), lambda k: (0, 0)),
        ],
        out_specs=pl.BlockSpec((tn, d), lambda k: (item(k, 1)[0], 0)),
        out_shape=jax.ShapeDtypeStruct((n, d), F32),
        scratch_shapes=[
            pltpu.VMEM((te, tn), F32),
            pltpu.VMEM((te, tn), F32),
            pltpu.VMEM((te, tn), BF16),
            pltpu.VMEM((d, tn), F32),
        ],
        compiler_params=_cparams(("arbitrary",)),
        name="peer_dense",
    )(xnt, u, vt, g1, n1, g2, r2, x1, final_norm)


def _layer(x2, batch, seq, norm_mix, w_in, conv_w, conv_b, w_a, w_b, w_o, norm_ffn, w_q, sub_keys,
           expert_u, expert_v, out_norm):
    d = x2.shape[1]
    qkv, cbo, gates = _inproj(x2, norm_mix.reshape(1, d), w_in.astype(BF16),
                              conv_w.reshape(CONV_K, -1), conv_b.reshape(1, -1), seq)
    att = _attention(qkv, batch, seq)
    keys = sub_keys.reshape(-1, N_KEYS, sub_keys.shape[-1]).astype(BF16)
    x1, xnt, st = _mid(att, cbo, gates, x2, w_a.astype(BF16), w_b.astype(BF16), w_o.astype(BF16),
                       norm_ffn.reshape(1, d), w_q.T.astype(BF16), keys)
    g1, n1, g2, r2 = _topk(st)
    return _peer(xnt, expert_u.astype(BF16), expert_v.T.astype(BF16), g1, n1, g2, r2, x1, out_norm)


def kernel(x, norm_mix, w_in, conv_w, conv_b, w_branch_a, w_branch_b, w_out, norm_ffn, w_q, sub_keys,
           expert_u, expert_v, final_norm):
    batch, seq, d = x.shape
    depth = w_in.shape[0]
    assert depth == 1, "the final RMSNorm is fused into the last layer's PEER kernel"
    x2 = x.reshape(batch * seq, d)
    out = _layer(x2, batch, seq, norm_mix[0], w_in[0], conv_w[0], conv_b[0], w_branch_a[0],
                 w_branch_b[0], w_out[0], norm_ffn[0], w_q[0], sub_keys[0], expert_u[0], expert_v[0],
                 final_norm.reshape(1, d))
    return out.reshape(batch, seq, d)
```

```python
import functools
import math

import jax
import jax.numpy as jnp
from jax import lax
from jax.experimental import pallas as pl
from jax.experimental.pallas import tpu as pltpu

F32 = jnp.float32
BF16 = jnp.bfloat16

RMS_EPS = 1e-6
SB_HEADS = 8
SB_HEAD_DIM = 64
SB_WIDTH = SB_HEADS * SB_HEAD_DIM
CONV_K = 3
PEER_HEADS = 8
N_KEYS = 128
PEER_TOPK = 16
LANES = 128
SUBLANES = 8
BF16_ROWS = 16
VMEM_LIMIT_BYTES = 56 * 1024 * 1024
EXP_UNDERFLOW_LOG = -110.0

IN_TM = 512
ATT_TQ = 256
ATT_TK = 256
ATT_WIDTH = 256
MID_TM = 512
TOPK_TN = 128
PEER_TN = 512
PEER_TE = 2048
PEER_LANE_CHUNK = 256


def _cparams(sem):
    return pltpu.CompilerParams(dimension_semantics=sem, vmem_limit_bytes=VMEM_LIMIT_BYTES)


def _inproj_kernel(x_ref, nm_ref, w_ref, cw_ref, cb_ref, qkv_ref, cbo_ref, gate_ref, carry_ref,
                   *, tiles_per_seq, width):
    i = pl.program_id(0)
    x = x_ref[...]
    ms = jnp.mean(x * x, axis=-1, keepdims=True)
    h = (x * lax.rsqrt(ms + RMS_EPS) * nm_ref[...]).astype(BF16)
    o3 = 3 * width
    qkv_ref[...] = jnp.dot(h, w_ref[:, 0:o3], preferred_element_type=F32).astype(BF16)

    c = jnp.dot(h, w_ref[:, o3:2 * o3], preferred_element_type=F32)
    cb, cc, cu = c[:, 0:width], c[:, width:2 * width], c[:, 2 * width:3 * width]
    z = cc * cu
    tm = z.shape[0]

    @pl.when(i % tiles_per_seq == 0)
    def _():
        carry_ref[...] = jnp.zeros_like(carry_ref)

    prev = carry_ref[...]
    rows = lax.broadcasted_iota(jnp.int32, z.shape, 0)
    z1 = jnp.where(rows == 0, prev[7:8, :], pltpu.roll(z, 1, 0))
    z2 = jnp.where(rows == 0, prev[6:7, :], jnp.where(rows == 1, prev[7:8, :], pltpu.roll(z, 2, 0)))
    y = cw_ref[0:1, :] * z2 + cw_ref[1:2, :] * z1 + cw_ref[2:3, :] * z + cb_ref[...]
    cbo_ref[...] = (cb * y).astype(BF16)
    carry_ref[...] = z[tm - SUBLANES:, :]

    g = jnp.dot(h, w_ref[:, 2 * o3:], preferred_element_type=F32)
    gate_ref[...] = 0.5 * jnp.tanh(0.5 * g) + 0.5


def _inproj(x2, norm_mix, w_in, conv_w, conv_b, seq):
    n, d = x2.shape
    width = SB_WIDTH
    in_width = w_in.shape[1]
    tm = IN_TM
    kern = functools.partial(_inproj_kernel, tiles_per_seq=seq // tm, width=width)
    return pl.pallas_call(
        kern,
        grid=(n // tm,),
        in_specs=[
            pl.BlockSpec((tm, d), lambda i: (i, 0)),
            pl.BlockSpec((1, d), lambda i: (0, 0)),
            pl.BlockSpec((d, in_width), lambda i: (0, 0)),
            pl.BlockSpec((CONV_K, width), lambda i: (0, 0)),
            pl.BlockSpec((1, width), lambda i: (0, 0)),
        ],
        out_specs=[
            pl.BlockSpec((tm, 3 * width), lambda i: (i, 0)),
            pl.BlockSpec((tm, width), lambda i: (i, 0)),
            pl.BlockSpec((tm, 2 * d), lambda i: (i, 0)),
        ],
        out_shape=[
            jax.ShapeDtypeStruct((n, 3 * width), BF16),
            jax.ShapeDtypeStruct((n, width), BF16),
            jax.ShapeDtypeStruct((n, 2 * d), F32),
        ],
        scratch_shapes=[pltpu.VMEM((SUBLANES, width), F32)],
        compiler_params=_cparams(("arbitrary",)),
        name="inproj",
    )(x2, norm_mix, w_in, conv_w, conv_b)


def _att_weights(qh, kj, tri, c, mask):
    z = lax.dot_general(qh, kj, (((1,), (1,)), ((), ())), preferred_element_type=F32)
    sp = jnp.maximum(z, 0.0) + jnp.log(1.0 + jnp.exp(-jnp.abs(z)))
    l1mb = -sp
    if mask is not None:
        l1mb = jnp.where(mask, l1mb, 0.0)
    hi = l1mb.astype(BF16)
    lo = (l1mb - hi.astype(F32)).astype(BF16)
    excl = (jnp.dot(hi, tri, preferred_element_type=F32)
            + jnp.dot(lo, tri, preferred_element_type=F32))
    a = jnp.exp((z - sp) + excl + c)
    if mask is not None:
        a = jnp.where(mask, a, 0.0)
    return a.astype(BF16), jnp.sum(l1mb, axis=-1, keepdims=True)


def _attention_kernel(q_ref, k_ref, v_ref, o_ref, *, tq, tk, scale):
    i = pl.program_id(2)
    width = q_ref.shape[1]
    heads = width // SB_HEAD_DIM
    q = q_ref[...] * jnp.asarray(scale, BF16)
    kr = lax.broadcasted_iota(jnp.int32, (tk, tk), 0)
    kc = lax.broadcasted_iota(jnp.int32, (tk, tk), 1)
    tri = jnp.where(kr > kc, 1.0, 0.0).astype(BF16)
    qr = lax.broadcasted_iota(jnp.int32, (tq, tk), 0)
    qc = lax.broadcasted_iota(jnp.int32, (tq, tk), 1)
    diag_mask = qc < qr

    head_of_lane = lax.broadcasted_iota(jnp.int32, (tq, width), 1) // SB_HEAD_DIM
    zero = jnp.zeros_like(q)
    qs = [jnp.where(head_of_lane == h, q, zero) for h in range(heads)]

    def all_heads(j, cs, o, mask):
        s = pl.multiple_of(j * tk, tk)
        kj = k_ref[pl.ds(s, tk), :]
        vj = v_ref[pl.ds(s, tk), :]
        weights, totals = zip(*[_att_weights(qs[h], kj, tri, cs[h], mask) for h in range(heads)])
        v_heads = jnp.concatenate([jnp.where(head_of_lane == h, vj, zero) for h in range(heads)], axis=0)
        o = o + jnp.dot(jnp.concatenate(weights, axis=1), v_heads, preferred_element_type=F32)
        return tuple(c + t for c, t in zip(cs, totals)), o

    def alive(cs):
        return (jnp.max(functools.reduce(jnp.maximum, cs)) >= EXP_UNDERFLOW_LOG).astype(jnp.int32)

    c0 = (jnp.zeros((tq, 1), F32),) * heads
    o0 = jnp.zeros((tq, width), F32)

    @pl.when(i == 0)
    def _():
        _, o = all_heads(i, c0, o0, diag_mask)
        o_ref[...] = o.astype(o_ref.dtype)

    @pl.when(i > 0)
    def _():
        cs, o = all_heads(i, c0, o0, diag_mask)
        cs, o = all_heads(i - 1, cs, o, None)

        def cond(state):
            jj, _, _, live = state
            return jnp.logical_and(jj <= i, live > 0)

        def body(state):
            jj, cs, o, _ = state
            cs, o = all_heads(i - jj, cs, o, None)
            return jj + 1, cs, o, alive(cs)

        _, _, o, _ = lax.while_loop(cond, body, (jnp.int32(2), cs, o, alive(cs)))
        o_ref[...] = o.astype(o_ref.dtype)


def _attention(qkv, batch, seq):
    n = qkv.shape[0]
    tq, tk = ATT_TQ, ATT_TK
    assert tq == tk
    width = ATT_WIDTH
    groups = SB_WIDTH // width
    nq = seq // tq
    kern = functools.partial(_attention_kernel, tq=tq, tk=tk, scale=SB_HEAD_DIM ** -0.5)
    return pl.pallas_call(
        kern,
        grid=(batch, groups, nq),
        in_specs=[
            pl.BlockSpec((tq, width), lambda b, p, i: (b * nq + i, p)),
            pl.BlockSpec((seq, width), lambda b, p, i: (b, groups + p)),
            pl.BlockSpec((seq, width), lambda b, p, i: (b, 2 * groups + p)),
        ],
        out_specs=pl.BlockSpec((tq, width), lambda b, p, i: (b * nq + i, p)),
        out_shape=jax.ShapeDtypeStruct((n, SB_WIDTH), BF16),
        compiler_params=_cparams(("arbitrary", "arbitrary", "arbitrary")),
        name="stickbreak_attention",
    )(qkv, qkv, qkv)


def _mid_kernel(att_ref, cbo_ref, gate_ref, x_ref, wa_ref, wb_ref, wo_ref, nf_ref, wqt_ref, keys_ref,
                x1_ref, xnt_ref, st_ref):
    d = x_ref.shape[1]
    ya = jnp.dot(att_ref[...], wa_ref[...], preferred_element_type=F32)
    yb = jnp.dot(cbo_ref[...], wb_ref[...], preferred_element_type=F32)
    merged = (gate_ref[:, 0:d] * ya + gate_ref[:, d:2 * d] * yb).astype(BF16)
    x1 = x_ref[...] + jnp.dot(merged, wo_ref[...], preferred_element_type=F32)
    x1_ref[...] = x1
    ms = jnp.mean(x1 * x1, axis=-1, keepdims=True)
    xn = x1 * lax.rsqrt(ms + RMS_EPS) * nf_ref[...]
    xnt_ref[...] = xn.T.astype(BF16)
    qt = lax.dot_general(wqt_ref[...], xn.astype(BF16), (((1,), (1,)), ((), ())),
                         preferred_element_type=F32)
    for hp in range(st_ref.shape[0]):
        qhp = qt[hp * N_KEYS:(hp + 1) * N_KEYS, :].astype(BF16)
        st_ref[hp] = jnp.dot(keys_ref[hp], qhp, preferred_element_type=F32)


def _mid(att, cbo, gates, x2, wa, wb, wo, norm_ffn, wqt, keys):
    n, d = x2.shape
    tm = MID_TM
    nhp = keys.shape[0]
    const2 = lambda i: (0, 0)
    return pl.pallas_call(
        _mid_kernel,
        grid=(n // tm,),
        in_specs=[
            pl.BlockSpec((tm, SB_WIDTH), lambda i: (i, 0)),
            pl.BlockSpec((tm, SB_WIDTH), lambda i: (i, 0)),
            pl.BlockSpec((tm, 2 * d), lambda i: (i, 0)),
            pl.BlockSpec((tm, d), lambda i: (i, 0)),
            pl.BlockSpec(wa.shape, const2),
            pl.BlockSpec(wb.shape, const2),
            pl.BlockSpec(wo.shape, const2),
            pl.BlockSpec((1, d), const2),
            pl.BlockSpec(wqt.shape, const2),
            pl.BlockSpec(keys.shape, lambda i: (0, 0, 0)),
        ],
        out_specs=[
            pl.BlockSpec((tm, d), lambda i: (i, 0)),
            pl.BlockSpec((d, tm), lambda i: (0, i)),
            pl.BlockSpec((nhp, N_KEYS, tm), lambda i: (0, 0, i)),
        ],
        out_shape=[
            jax.ShapeDtypeStruct((n, d), F32),
            jax.ShapeDtypeStruct((d, n), BF16),
            jax.ShapeDtypeStruct((nhp, N_KEYS, n), F32),
        ],
        compiler_params=_cparams(("arbitrary",)),
        name="mid_proj",
    )(att, cbo, gates, x2, wa, wb, wo, norm_ffn, wqt, keys)


def _candidate_cells():
    return [(a, b) for a in range(PEER_TOPK) for b in range(PEER_TOPK) if (a + 1) * (b + 1) <= PEER_TOPK]


def _topk_kernel(st_ref, g1_ref, n1_ref, g2_ref, r2_ref, rank_scr, tops_scr):
    k = PEER_TOPK
    neg_inf = jnp.asarray(-jnp.inf, F32)
    kio = lax.broadcasted_iota(jnp.int32, (N_KEYS, LANES), 0).astype(F32)
    sub = lax.broadcasted_iota(jnp.int32, (SUBLANES, LANES), 0)

    def record_top(p, h, r, m):
        slot = p * k + r
        tops_scr[slot] = jnp.where(sub == h, m, tops_scr[slot])

    def extract_all_equal(h2, tied):
        for hp in range(4):
            h = 2 * h2 + hp // 2
            p = hp % 2
            x = st_ref[2 * h + p]
            peeled = []
            for r in range(k):
                m = jnp.max(x, axis=0, keepdims=True)
                x = jnp.where(x == m, neg_inf, x)
                record_top(p, h, r, m)
                peeled.append(m)
            x = st_ref[2 * h + p]
            rank = jnp.full((N_KEYS, LANES), float(k), F32)
            for r in range(k):
                rank = jnp.where(x == peeled[r], float(r), rank)
            rank_scr[2 * h + p] = rank
            removed = jnp.sum(jnp.where(rank < float(k), 1.0, 0.0), axis=0, keepdims=True)
            tied = jnp.maximum(tied, jnp.where(removed != float(k), 1.0, 0.0))
        return tied

    tied = lax.fori_loop(0, PEER_HEADS // 2, extract_all_equal, jnp.zeros((1, LANES), F32))

    @pl.when(jnp.max(tied) > 0.0)
    def _():
        def extract_first(hp, _):
            x = st_ref[hp]
            rank = jnp.full((N_KEYS, LANES), float(k), F32)
            h = hp // 2
            p = hp % 2
            for r in range(k):
                m = jnp.max(x, axis=0, keepdims=True)
                first = jnp.min(jnp.where(x == m, kio, float(N_KEYS)), axis=0, keepdims=True)
                hit = kio == first
                rank = jnp.where(hit, float(r), rank)
                x = jnp.where(hit, neg_inf, x)
                record_top(p, h, r, m)
            rank_scr[hp] = rank
            return 0

        lax.fori_loop(0, st_ref.shape[0], extract_first, 0)

    t1 = [tops_scr[a] for a in range(k)]
    t2 = [tops_scr[k + b] for b in range(k)]
    cells = _candidate_cells()
    val = {ab: t1[ab[0]] + t2[ab[1]] for ab in cells}
    beat = {ab: jnp.full((SUBLANES, LANES), float((ab[0] + 1) * (ab[1] + 1) - 1), F32) for ab in cells}
    for ip, p_ in enumerate(cells):
        for q_ in cells[ip + 1:]:
            if p_[0] <= q_[0] and p_[1] <= q_[1]:
                continue
            t = jnp.where(val[p_] >= val[q_], 1.0, 0.0)
            beat[q_] = beat[q_] + t
            beat[p_] = beat[p_] + (1.0 - t)
    top = val[(0, 0)]
    zsum = jnp.zeros((SUBLANES, LANES), F32)
    n_sel = [jnp.zeros((SUBLANES, LANES), F32) for _ in range(k)]
    for ab in cells:
        sel = beat[ab] < float(k)
        zsum = zsum + jnp.where(sel, jnp.exp(val[ab] - top), 0.0)
        n_sel[ab[0]] = n_sel[ab[0]] + jnp.where(sel, 1.0, 0.0)
    zinv = 0.5 / zsum

    for h in range(PEER_HEADS):
        rank1 = rank_scr[2 * h]
        rank2 = rank_scr[2 * h + 1]
        m1 = t1[0][h:h + 1, :]
        m2 = t2[0][h:h + 1, :]
        g1_ref[h] = jnp.where(rank1 < float(k), jnp.exp(st_ref[2 * h] - m1), 0.0)
        g2 = jnp.where(rank2 < float(k), jnp.exp(st_ref[2 * h + 1] - m2) * zinv[h:h + 1, :], 0.0)
        g2_ref[h] = g2.astype(BF16)
        r2_ref[h] = rank2.astype(BF16)
        n1 = jnp.zeros((N_KEYS, LANES), F32)
        for a in range(k):
            n1 = jnp.where(rank1 == float(a), n_sel[a][h:h + 1, :], n1)
        n1_ref[h] = n1


def _topk(st):
    nhp, nk, n = st.shape
    tn = TOPK_TN
    heads = nhp // 2
    spec = pl.BlockSpec((heads, nk, tn), lambda i: (0, 0, i))
    return pl.pallas_call(
        _topk_kernel,
        grid=(n // tn,),
        in_specs=[pl.BlockSpec((nhp, nk, tn), lambda i: (0, 0, i))],
        out_specs=[spec, spec, spec, spec],
        out_shape=[
            jax.ShapeDtypeStruct((heads, nk, n), F32),
            jax.ShapeDtypeStruct((heads, nk, n), F32),
            jax.ShapeDtypeStruct((heads, nk, n), BF16),
            jax.ShapeDtypeStruct((heads, nk, n), BF16),
        ],
        scratch_shapes=[
            pltpu.VMEM((nhp, nk, tn), F32),
            pltpu.VMEM((2 * PEER_TOPK, SUBLANES, tn), F32),
        ],
        compiler_params=_cparams(("arbitrary",)),
        name="peer_topk",
    )(st)


def _peer_tick(xnt_ref, u_ref, vt_ref, g1_ref, n1_ref, g2_ref, r2_ref, acc_scr, p_scr,
               s_write, s_read, first_of_token_tile):
    te, tn = s_read.shape
    s_write[...] = jnp.dot(u_ref[...], xnt_ref[...], preferred_element_type=F32)

    group = (BF16_ROWS, PEER_LANE_CHUNK)
    for j in range(te // N_KEYS):
        rows = slice(j * N_KEYS, (j + 1) * N_KEYS)
        for lc in range(tn // PEER_LANE_CHUNK):
            cols = slice(lc * PEER_LANE_CHUNK, (lc + 1) * PEER_LANE_CHUNK)
            w = jnp.zeros((N_KEYS // BF16_ROWS,) + group, BF16)
            for h in range(PEER_HEADS):
                g1 = jnp.broadcast_to(g1_ref[h, j:j + 1, cols], group).astype(BF16)
                n1 = jnp.broadcast_to(n1_ref[h, j:j + 1, cols], group).astype(BF16)
                w = w + jnp.where(r2_ref[h, :, :, cols] < n1, g2_ref[h, :, :, cols] * g1, jnp.zeros_like(w))
            s = s_read[rows, cols]
            act = s * (1.0 + lax.erf(s * math.sqrt(0.5)))
            p_scr[rows, cols] = act.astype(BF16) * w.reshape(N_KEYS, PEER_LANE_CHUNK)

    upd = jnp.dot(vt_ref[...], p_scr[...], preferred_element_type=F32)
    acc_scr[...] = jnp.where(first_of_token_tile, upd, acc_scr[...] + upd)


def _peer_kernel(xnt_ref, u_ref, vt_ref, g1_ref, n1_ref, g2_ref, r2_ref, x1_ref, fn_ref, out_ref,
                 s0, s1, p_scr, acc_scr, *, n_items, e_tiles):
    k = pl.program_id(0)
    e_b = jnp.clip(k - 1, 0, n_items - 1) % e_tiles

    @pl.when(k == 0)
    def _():
        s1[...] = jnp.zeros_like(s1)
        acc_scr[...] = jnp.zeros_like(acc_scr)

    refs = (xnt_ref, u_ref, vt_ref, g1_ref, n1_ref, g2_ref, r2_ref, acc_scr, p_scr)

    @pl.when(k % 2 == 0)
    def _():
        _peer_tick(*refs, s0, s1, e_b == 0)

    @pl.when(k % 2 == 1)
    def _():
        _peer_tick(*refs, s1, s0, e_b == 0)

    @pl.when(jnp.logical_and(k >= 1, e_b == e_tiles - 1))
    def _():
        y = x1_ref[...] + acc_scr[...].T
        ms = jnp.mean(y * y, axis=-1, keepdims=True)
        out_ref[...] = y * lax.rsqrt(ms + RMS_EPS) * fn_ref[...]


def _peer(xnt, u, vt, g1, n1, g2, r2, x1, final_norm):
    d, n = xnt.shape
    n_exp = u.shape[0]
    tn, te = PEER_TN, PEER_TE
    heads = g1.shape[0]
    i1_per_tile = te // N_KEYS
    assert i1_per_tile % SUBLANES == 0
    row_groups = N_KEYS // BF16_ROWS
    g2 = g2.reshape(heads, row_groups, BF16_ROWS, n)
    r2 = r2.reshape(heads, row_groups, BF16_ROWS, n)
    e_tiles = n_exp // te
    n_items = (n // tn) * e_tiles

    def item(k, lag):
        it = jnp.clip(k - lag, 0, n_items - 1)
        return it // e_tiles, it % e_tiles

    kern = functools.partial(_peer_kernel, n_items=n_items, e_tiles=e_tiles)
    return pl.pallas_call(
        kern,
        grid=(n_items + 1,),
        in_specs=[
            pl.BlockSpec((d, tn), lambda k: (0, item(k, 0)[0])),
            pl.BlockSpec((te, d), lambda k: (item(k, 0)[1], 0)),
            pl.BlockSpec((d, te), lambda k: (0, item(k, 1)[1])),
            pl.BlockSpec((heads, i1_per_tile, tn), lambda k: (0, item(k, 1)[1], item(k, 1)[0])),
            pl.BlockSpec((heads, i1_per_tile, tn), lambda k: (0, item(k, 1)[1], item(k, 1)[0])),
            pl.BlockSpec((heads, row_groups, BF16_ROWS, tn), lambda k: (0, 0, 0, item(k, 1)[0])),
            pl.BlockSpec((heads, row_groups, BF16_ROWS, tn), lambda k: (0, 0, 0, item(k, 1)[0])),
            pl.BlockSpec((tn, d), lambda k: (item(k, 1)[0], 0)),
            pl.BlockSpec((1, d), lambda k: (0, 0)),
        ],
        out_specs=pl.BlockSpec((tn, d), lambda k: (item(k, 1)[0], 0)),
        out_shape=jax.ShapeDtypeStruct((n, d), F32),
        scratch_shapes=[
            pltpu.VMEM((te, tn), F32),
            pltpu.VMEM((te, tn), F32),
            pltpu.VMEM((te, tn), BF16),
            pltpu.VMEM((d, tn), F32),
        ],
        compiler_params=_cparams(("arbitrary",)),
        name="peer_dense",
    )(xnt, u, vt, g1, n1, g2, r2, x1, final_norm)


def _layer(x2, batch, seq, norm_mix, w_in, conv_w, conv_b, w_a, w_b, w_o, norm_ffn, w_q, sub_keys,
           expert_u, expert_v, out_norm):
    d = x2.shape[1]
    qkv, cbo, gates = _inproj(x2, norm_mix.reshape(1, d), w_in.astype(BF16),
                              conv_w.reshape(CONV_K, -1), conv_b.reshape(1, -1), seq)
    att = _attention(qkv, batch, seq)
    keys = sub_keys.reshape(-1, N_KEYS, sub_keys.shape[-1]).astype(BF16)
    x1, xnt, st = _mid(att, cbo, gates, x2, w_a.astype(BF16), w_b.astype(BF16), w_o.astype(BF16),
                       norm_ffn.reshape(1, d), w_q.T.astype(BF16), keys)
    g1, n1, g2, r2 = _topk(st)
    return _peer(xnt, expert_u.astype(BF16), expert_v.T.astype(BF16), g1, n1, g2, r2, x1, out_norm)


def kernel(x, norm_mix, w_in, conv_w, conv_b, w_branch_a, w_branch_b, w_out, norm_ffn, w_q, sub_keys,
           expert_u, expert_v, final_norm):
    batch, seq, d = x.shape
    depth = w_in.shape[0]
    assert depth == 1, "the final RMSNorm is fused into the last layer's PEER kernel"
    x2 = x.reshape(batch * seq, d)
    out = _layer(x2, batch, seq, norm_mix[0], w_in[0], conv_w[0], conv_b[0], w_branch_a[0],
                 w_branch_b[0], w_out[0], norm_ffn[0], w_q[0], sub_keys[0], expert_u[0], expert_v[0],
                 final_norm.reshape(1, d))
    return out.reshape(batch, seq, d)
```

```python
import functools
import math

import jax
import jax.numpy as jnp
from jax import lax
from jax.experimental import pallas as pl
from jax.experimental.pallas import tpu as pltpu

F32 = jnp.float32
BF16 = jnp.bfloat16

RMS_EPS = 1e-6
SB_HEADS = 8
SB_HEAD_DIM = 64
SB_WIDTH = SB_HEADS * SB_HEAD_DIM
CONV_K = 3
PEER_HEADS = 8
N_KEYS = 128
PEER_TOPK = 16
LANES = 128
SUBLANES = 8
BF16_ROWS = 16
VMEM_LIMIT_BYTES = 56 * 1024 * 1024
EXP_UNDERFLOW_LOG = -110.0

IN_TM = 512
ATT_TQ = 256
ATT_TK = 256
ATT_WIDTH = 256
MID_TM = 512
TOPK_TN = 128
PEER_TN = 512
PEER_TE = 1024
PEER_LANE_CHUNK = 256


def _cparams(sem):
    return pltpu.CompilerParams(dimension_semantics=sem, vmem_limit_bytes=VMEM_LIMIT_BYTES)


def _inproj_kernel(x_ref, nm_ref, w_ref, cw_ref, cb_ref, qkv_ref, cbo_ref, gate_ref, carry_ref,
                   *, tiles_per_seq, width):
    i = pl.program_id(0)
    x = x_ref[...]
    ms = jnp.mean(x * x, axis=-1, keepdims=True)
    h = (x * lax.rsqrt(ms + RMS_EPS) * nm_ref[...]).astype(BF16)
    o3 = 3 * width
    qkv_ref[...] = jnp.dot(h, w_ref[:, 0:o3], preferred_element_type=F32).astype(BF16)

    c = jnp.dot(h, w_ref[:, o3:2 * o3], preferred_element_type=F32)
    cb, cc, cu = c[:, 0:width], c[:, width:2 * width], c[:, 2 * width:3 * width]
    z = cc * cu
    tm = z.shape[0]

    @pl.when(i % tiles_per_seq == 0)
    def _():
        carry_ref[...] = jnp.zeros_like(carry_ref)

    prev = carry_ref[...]
    rows = lax.broadcasted_iota(jnp.int32, z.shape, 0)
    z1 = jnp.where(rows == 0, prev[7:8, :], pltpu.roll(z, 1, 0))
    z2 = jnp.where(rows == 0, prev[6:7, :], jnp.where(rows == 1, prev[7:8, :], pltpu.roll(z, 2, 0)))
    y = cw_ref[0:1, :] * z2 + cw_ref[1:2, :] * z1 + cw_ref[2:3, :] * z + cb_ref[...]
    cbo_ref[...] = (cb * y).astype(BF16)
    carry_ref[...] = z[tm - SUBLANES:, :]

    g = jnp.dot(h, w_ref[:, 2 * o3:], preferred_element_type=F32)
    gate_ref[...] = 0.5 * jnp.tanh(0.5 * g) + 0.5


def _inproj(x2, norm_mix, w_in, conv_w, conv_b, seq):
    n, d = x2.shape
    width = SB_WIDTH
    in_width = w_in.shape[1]
    tm = IN_TM
    kern = functools.partial(_inproj_kernel, tiles_per_seq=seq // tm, width=width)
    return pl.pallas_call(
        kern,
        grid=(n // tm,),
        in_specs=[
            pl.BlockSpec((tm, d), lambda i: (i, 0)),
            pl.BlockSpec((1, d), lambda i: (0, 0)),
            pl.BlockSpec((d, in_width), lambda i: (0, 0)),
            pl.BlockSpec((CONV_K, width), lambda i: (0, 0)),
            pl.BlockSpec((1, width), lambda i: (0, 0)),
        ],
        out_specs=[
            pl.BlockSpec((tm, 3 * width), lambda i: (i, 0)),
            pl.BlockSpec((tm, width), lambda i: (i, 0)),
            pl.BlockSpec((tm, 2 * d), lambda i: (i, 0)),
        ],
        out_shape=[
            jax.ShapeDtypeStruct((n, 3 * width), BF16),
            jax.ShapeDtypeStruct((n, width), BF16),
            jax.ShapeDtypeStruct((n, 2 * d), F32),
        ],
        scratch_shapes=[pltpu.VMEM((SUBLANES, width), F32)],
        compiler_params=_cparams(("arbitrary",)),
        name="inproj",
    )(x2, norm_mix, w_in, conv_w, conv_b)


def _att_weights(qh, kj, tri, c, mask):
    z = lax.dot_general(qh, kj, (((1,), (1,)), ((), ())), preferred_element_type=F32)
    sp = jnp.maximum(z, 0.0) + jnp.log(1.0 + jnp.exp(-jnp.abs(z)))
    l1mb = -sp
    if mask is not None:
        l1mb = jnp.where(mask, l1mb, 0.0)
    hi = l1mb.astype(BF16)
    lo = (l1mb - hi.astype(F32)).astype(BF16)
    excl = (jnp.dot(hi, tri, preferred_element_type=F32)
            + jnp.dot(lo, tri, preferred_element_type=F32))
    a = jnp.exp((z - sp) + excl + c)
    if mask is not None:
        a = jnp.where(mask, a, 0.0)
    return a.astype(BF16), jnp.sum(l1mb, axis=-1, keepdims=True)


def _attention_kernel(q_ref, k_ref, v_ref, o_ref, *, tq, tk, scale):
    i = pl.program_id(2)
    width = q_ref.shape[1]
    heads = width // SB_HEAD_DIM
    q = q_ref[...] * jnp.asarray(scale, BF16)
    kr = lax.broadcasted_iota(jnp.int32, (tk, tk), 0)
    kc = lax.broadcasted_iota(jnp.int32, (tk, tk), 1)
    tri = jnp.where(kr > kc, 1.0, 0.0).astype(BF16)
    qr = lax.broadcasted_iota(jnp.int32, (tq, tk), 0)
    qc = lax.broadcasted_iota(jnp.int32, (tq, tk), 1)
    diag_mask = qc < qr

    head_of_lane = lax.broadcasted_iota(jnp.int32, (tq, width), 1) // SB_HEAD_DIM
    zero = jnp.zeros_like(q)
    qs = [jnp.where(head_of_lane == h, q, zero) for h in range(heads)]

    def all_heads(j, cs, o, mask):
        s = pl.multiple_of(j * tk, tk)
        kj = k_ref[pl.ds(s, tk), :]
        vj = v_ref[pl.ds(s, tk), :]
        weights, totals = zip(*[_att_weights(qs[h], kj, tri, cs[h], mask) for h in range(heads)])
        v_heads = jnp.concatenate([jnp.where(head_of_lane == h, vj, zero) for h in range(heads)], axis=0)
        o = o + jnp.dot(jnp.concatenate(weights, axis=1), v_heads, preferred_element_type=F32)
        return tuple(c + t for c, t in zip(cs, totals)), o

    def alive(cs):
        return (jnp.max(functools.reduce(jnp.maximum, cs)) >= EXP_UNDERFLOW_LOG).astype(jnp.int32)

    c0 = (jnp.zeros((tq, 1), F32),) * heads
    o0 = jnp.zeros((tq, width), F32)

    @pl.when(i == 0)
    def _():
        _, o = all_heads(i, c0, o0, diag_mask)
        o_ref[...] = o.astype(o_ref.dtype)

    @pl.when(i > 0)
    def _():
        cs, o = all_heads(i, c0, o0, diag_mask)
        cs, o = all_heads(i - 1, cs, o, None)

        def cond(state):
            jj, _, _, live = state
            return jnp.logical_and(jj <= i, live > 0)

        def body(state):
            jj, cs, o, _ = state
            cs, o = all_heads(i - jj, cs, o, None)
            return jj + 1, cs, o, alive(cs)

        _, _, o, _ = lax.while_loop(cond, body, (jnp.int32(2), cs, o, alive(cs)))
        o_ref[...] = o.astype(o_ref.dtype)


def _attention(qkv, batch, seq):
    n = qkv.shape[0]
    tq, tk = ATT_TQ, ATT_TK
    assert tq == tk
    width = ATT_WIDTH
    groups = SB_WIDTH // width
    nq = seq // tq
    kern = functools.partial(_attention_kernel, tq=tq, tk=tk, scale=SB_HEAD_DIM ** -0.5)
    return pl.pallas_call(
        kern,
        grid=(batch, groups, nq),
        in_specs=[
            pl.BlockSpec((tq, width), lambda b, p, i: (b * nq + i, p)),
            pl.BlockSpec((seq, width), lambda b, p, i: (b, groups + p)),
            pl.BlockSpec((seq, width), lambda b, p, i: (b, 2 * groups + p)),
        ],
        out_specs=pl.BlockSpec((tq, width), lambda b, p, i: (b * nq + i, p)),
        out_shape=jax.ShapeDtypeStruct((n, SB_WIDTH), BF16),
        compiler_params=_cparams(("arbitrary", "arbitrary", "arbitrary")),
        name="stickbreak_attention",
    )(qkv, qkv, qkv)


def _mid_kernel(att_ref, cbo_ref, gate_ref, x_ref, wa_ref, wb_ref, wo_ref, nf_ref, wqt_ref, keys_ref,
                x1_ref, xnt_ref, st_ref):
    d = x_ref.shape[1]
    ya = jnp.dot(att_ref[...], wa_ref[...], preferred_element_type=F32)
    yb = jnp.dot(cbo_ref[...], wb_ref[...], preferred_element_type=F32)
    merged = (gate_ref[:, 0:d] * ya + gate_ref[:, d:2 * d] * yb).astype(BF16)
    x1 = x_ref[...] + jnp.dot(merged, wo_ref[...], preferred_element_type=F32)
    x1_ref[...] = x1
    ms = jnp.mean(x1 * x1, axis=-1, keepdims=True)
    xn = x1 * lax.rsqrt(ms + RMS_EPS) * nf_ref[...]
    xnt_ref[...] = xn.T.astype(BF16)
    qt = lax.dot_general(wqt_ref[...], xn.astype(BF16), (((1,), (1,)), ((), ())),
                         preferred_element_type=F32)
    for hp in range(st_ref.shape[0]):
        qhp = qt[hp * N_KEYS:(hp + 1) * N_KEYS, :].astype(BF16)
        st_ref[hp] = jnp.dot(keys_ref[hp], qhp, preferred_element_type=F32)


def _mid(att, cbo, gates, x2, wa, wb, wo, norm_ffn, wqt, keys):
    n, d = x2.shape
    tm = MID_TM
    nhp = keys.shape[0]
    const2 = lambda i: (0, 0)
    return pl.pallas_call(
        _mid_kernel,
        grid=(n // tm,),
        in_specs=[
            pl.BlockSpec((tm, SB_WIDTH), lambda i: (i, 0)),
            pl.BlockSpec((tm, SB_WIDTH), lambda i: (i, 0)),
            pl.BlockSpec((tm, 2 * d), lambda i: (i, 0)),
            pl.BlockSpec((tm, d), lambda i: (i, 0)),
            pl.BlockSpec(wa.shape, const2),
            pl.BlockSpec(wb.shape, const2),
            pl.BlockSpec(wo.shape, const2),
            pl.BlockSpec((1, d), const2),
            pl.BlockSpec(wqt.shape, const2),
            pl.BlockSpec(keys.shape, lambda i: (0, 0, 0)),
        ],
        out_specs=[
            pl.BlockSpec((tm, d), lambda i: (i, 0)),
            pl.BlockSpec((d, tm), lambda i: (0, i)),
            pl.BlockSpec((nhp, N_KEYS, tm), lambda i: (0, 0, i)),
        ],
        out_shape=[
            jax.ShapeDtypeStruct((n, d), F32),
            jax.ShapeDtypeStruct((d, n), BF16),
            jax.ShapeDtypeStruct((nhp, N_KEYS, n), F32),
        ],
        compiler_params=_cparams(("arbitrary",)),
        name="mid_proj",
    )(att, cbo, gates, x2, wa, wb, wo, norm_ffn, wqt, keys)


def _candidate_cells():
    return [(a, b) for a in range(PEER_TOPK) for b in range(PEER_TOPK) if (a + 1) * (b + 1) <= PEER_TOPK]


def _topk_kernel(st_ref, g1_ref, n1_ref, g2_ref, r2_ref, rank_scr, tops_scr):
    k = PEER_TOPK
    neg_inf = jnp.asarray(-jnp.inf, F32)
    kio = lax.broadcasted_iota(jnp.int32, (N_KEYS, LANES), 0).astype(F32)
    sub = lax.broadcasted_iota(jnp.int32, (SUBLANES, LANES), 0)

    def record_top(p, h, r, m):
        slot = p * k + r
        tops_scr[slot] = jnp.where(sub == h, m, tops_scr[slot])

    def extract_all_equal(h2, tied):
        for hp in range(4):
            h = 2 * h2 + hp // 2
            p = hp % 2
            x = st_ref[2 * h + p]
            peeled = []
            for r in range(k):
                m = jnp.max(x, axis=0, keepdims=True)
                x = jnp.where(x == m, neg_inf, x)
                record_top(p, h, r, m)
                peeled.append(m)
            x = st_ref[2 * h + p]
            rank = jnp.full((N_KEYS, LANES), float(k), F32)
            for r in range(k):
                rank = jnp.where(x == peeled[r], float(r), rank)
            rank_scr[2 * h + p] = rank
            removed = jnp.sum(jnp.where(rank < float(k), 1.0, 0.0), axis=0, keepdims=True)
            tied = jnp.maximum(tied, jnp.where(removed != float(k), 1.0, 0.0))
        return tied

    tied = lax.fori_loop(0, PEER_HEADS // 2, extract_all_equal, jnp.zeros((1, LANES), F32))

    @pl.when(jnp.max(tied) > 0.0)
    def _():
        def extract_first(hp, _):
            x = st_ref[hp]
            rank = jnp.full((N_KEYS, LANES), float(k), F32)
            h = hp // 2
            p = hp % 2
            for r in range(k):
                m = jnp.max(x, axis=0, keepdims=True)
                first = jnp.min(jnp.where(x == m, kio, float(N_KEYS)), axis=0, keepdims=True)
                hit = kio == first
                rank = jnp.where(hit, float(r), rank)
                x = jnp.where(hit, neg_inf, x)
                record_top(p, h, r, m)
            rank_scr[hp] = rank
            return 0

        lax.fori_loop(0, st_ref.shape[0], extract_first, 0)

    t1 = [tops_scr[a] for a in range(k)]
    t2 = [tops_scr[k + b] for b in range(k)]
    cells = _candidate_cells()
    val = {ab: t1[ab[0]] + t2[ab[1]] for ab in cells}
    beat = {ab: jnp.full((SUBLANES, LANES), float((ab[0] + 1) * (ab[1] + 1) - 1), F32) for ab in cells}
    for ip, p_ in enumerate(cells):
        for q_ in cells[ip + 1:]:
            if p_[0] <= q_[0] and p_[1] <= q_[1]:
                continue
            t = jnp.where(val[p_] >= val[q_], 1.0, 0.0)
            beat[q_] = beat[q_] + t
            beat[p_] = beat[p_] + (1.0 - t)
    top = val[(0, 0)]
    zsum = jnp.zeros((SUBLANES, LANES), F32)
    n_sel = [jnp.zeros((SUBLANES, LANES), F32) for _ in range(k)]
    for ab in cells:
        sel = beat[ab] < float(k)
        zsum = zsum + jnp.where(sel, jnp.exp(val[ab] - top), 0.0)
        n_sel[ab[0]] = n_sel[ab[0]] + jnp.where(sel, 1.0, 0.0)
    zinv = 0.5 / zsum

    for h in range(PEER_HEADS):
        rank1 = rank_scr[2 * h]
        rank2 = rank_scr[2 * h + 1]
        m1 = t1[0][h:h + 1, :]
        m2 = t2[0][h:h + 1, :]
        g1_ref[h] = jnp.where(rank1 < float(k), jnp.exp(st_ref[2 * h] - m1), 0.0)
        g2 = jnp.where(rank2 < float(k), jnp.exp(st_ref[2 * h + 1] - m2) * zinv[h:h + 1, :], 0.0)
        g2_ref[h] = g2.astype(BF16)
        r2_ref[h] = rank2.astype(BF16)
        n1 = jnp.zeros((N_KEYS, LANES), F32)
        for a in range(k):
            n1 = jnp.where(rank1 == float(a), n_sel[a][h:h + 1, :], n1)
        n1_ref[h] = n1


def _topk(st):
    nhp, nk, n = st.shape
    tn = TOPK_TN
    heads = nhp // 2
    spec = pl.BlockSpec((heads, nk, tn), lambda i: (0, 0, i))
    return pl.pallas_call(
        _topk_kernel,
        grid=(n // tn,),
        in_specs=[pl.BlockSpec((nhp, nk, tn), lambda i: (0, 0, i))],
        out_specs=[spec, spec, spec, spec],
        out_shape=[
            jax.ShapeDtypeStruct((heads, nk, n), F32),
            jax.ShapeDtypeStruct((heads, nk, n), F32),
            jax.ShapeDtypeStruct((heads, nk, n), BF16),
            jax.ShapeDtypeStruct((heads, nk, n), BF16),
        ],
        scratch_shapes=[
            pltpu.VMEM((nhp, nk, tn), F32),
            pltpu.VMEM((2 * PEER_TOPK, SUBLANES, tn), F32),
        ],
        compiler_params=_cparams(("arbitrary",)),
        name="peer_topk",
    )(st)


def _peer_tick(xnt_ref, u_ref, vt_ref, g1_ref, n1_ref, g2_ref, r2_ref, acc_scr, p_scr,
               s_write, s_read, first_of_token_tile):
    te, tn = s_read.shape
    s_write[...] = jnp.dot(u_ref[...], xnt_ref[...], preferred_element_type=F32)

    group = (BF16_ROWS, PEER_LANE_CHUNK)
    for j in range(te // N_KEYS):
        rows = slice(j * N_KEYS, (j + 1) * N_KEYS)
        for lc in range(tn // PEER_LANE_CHUNK):
            cols = slice(lc * PEER_LANE_CHUNK, (lc + 1) * PEER_LANE_CHUNK)
            w = jnp.zeros((N_KEYS // BF16_ROWS,) + group, BF16)
            for h in range(PEER_HEADS):
                g1 = jnp.broadcast_to(g1_ref[h, j:j + 1, cols], group).astype(BF16)
                n1 = jnp.broadcast_to(n1_ref[h, j:j + 1, cols], group).astype(BF16)
                w = w + jnp.where(r2_ref[h, :, :, cols] < n1, g2_ref[h, :, :, cols] * g1, jnp.zeros_like(w))
            s = s_read[rows, cols]
            act = s * (1.0 + lax.erf(s * math.sqrt(0.5)))
            p_scr[rows, cols] = act.astype(BF16) * w.reshape(N_KEYS, PEER_LANE_CHUNK)

    upd = jnp.dot(vt_ref[...], p_scr[...], preferred_element_type=F32)
    acc_scr[...] = jnp.where(first_of_token_tile, upd, acc_scr[...] + upd)


def _peer_kernel(xnt_ref, u_ref, vt_ref, g1_ref, n1_ref, g2_ref, r2_ref, x1_ref, fn_ref, out_ref,
                 s0, s1, p_scr, acc_scr, *, n_items, e_tiles):
    k = pl.program_id(0)
    e_b = jnp.clip(k - 1, 0, n_items - 1) % e_tiles

    @pl.when(k == 0)
    def _():
        s1[...] = jnp.zeros_like(s1)
        acc_scr[...] = jnp.zeros_like(acc_scr)

    refs = (xnt_ref, u_ref, vt_ref, g1_ref, n1_ref, g2_ref, r2_ref, acc_scr, p_scr)

    @pl.when(k % 2 == 0)
    def _():
        _peer_tick(*refs, s0, s1, e_b == 0)

    @pl.when(k % 2 == 1)
    def _():
        _peer_tick(*refs, s1, s0, e_b == 0)

    @pl.when(jnp.logical_and(k >= 1, e_b == e_tiles - 1))
    def _():
        y = x1_ref[...] + acc_scr[...].T
        ms = jnp.mean(y * y, axis=-1, keepdims=True)
        out_ref[...] = y * lax.rsqrt(ms + RMS_EPS) * fn_ref[...]


def _peer(xnt, u, vt, g1, n1, g2, r2, x1, final_norm):
    d, n = xnt.shape
    n_exp = u.shape[0]
    tn, te = PEER_TN, PEER_TE
    heads = g1.shape[0]
    i1_per_tile = te // N_KEYS
    assert i1_per_tile % SUBLANES == 0
    row_groups = N_KEYS // BF16_ROWS
    g2 = g2.reshape(heads, row_groups, BF16_ROWS, n)
    r2 = r2.reshape(heads, row_groups, BF16_ROWS, n)
    e_tiles = n_exp // te
    n_items = (n // tn) * e_tiles

    def item(k, lag):
        it = jnp.clip(k - lag, 0, n_items - 1)
        return it // e_tiles, it % e_tiles

    kern = functools.partial(_peer_kernel, n_items=n_items, e_tiles=e_tiles)
    return pl.pallas_call(
        kern,
        grid=(n_items + 1,),
        in_specs=[
            pl.BlockSpec((d, tn), lambda k: (0, item(k, 0)[0])),
            pl.BlockSpec((te, d), lambda k: (item(k, 0)[1], 0)),
            pl.BlockSpec((d, te), lambda k: (0, item(k, 1)[1])),
            pl.BlockSpec((heads, i1_per_tile, tn), lambda k: (0, item(k, 1)[1], item(k, 1)[0])),
            pl.BlockSpec((heads, i1_per_tile, tn), lambda k: (0, item(k, 1)[1], item(k, 1)[0])),
            pl.BlockSpec((heads, row_groups, BF16_ROWS, tn), lambda k: (0, 0, 0, item(k, 1)[0])),
            pl.BlockSpec((heads, row_groups, BF16_ROWS, tn), lambda k: (0, 0, 0, item(k, 1)[0])),
            pl.BlockSpec((tn, d), lambda k: (item(k, 1)[0], 0)),
            pl.BlockSpec((1, d), lambda k: (0, 0)),
        ],
        out_specs=pl.BlockSpec((tn, d), lambda k: (item(k, 1)[0], 0)),
        out_shape=jax.ShapeDtypeStruct((n, d), F32),
        scratch_shapes=[
            pltpu.VMEM((te, tn), F32),
            pltpu.VMEM((te, tn), F32),
            pltpu.VMEM((te, tn), BF16),
            pltpu.VMEM((d, tn), F32),
        ],
        compiler_params=_cparams(("arbitrary",)),
        name="peer_dense",
    )(xnt, u, vt, g1, n1, g2, r2, x1, final_norm)


def _layer(x2, batch, seq, norm_mix, w_in, conv_w, conv_b, w_a, w_b, w_o, norm_ffn, w_q, sub_keys,
           expert_u, expert_v, out_norm):
    d = x2.shape[1]
    qkv, cbo, gates = _inproj(x2, norm_mix.reshape(1, d), w_in.astype(BF16),
                              conv_w.reshape(CONV_K, -1), conv_b.reshape(1, -1), seq)
    att = _attention(qkv, batch, seq)
    keys = sub_keys.reshape(-1, N_KEYS, sub_keys.shape[-1]).astype(BF16)
    x1, xnt, st = _mid(att, cbo, gates, x2, w_a.astype(BF16), w_b.astype(BF16), w_o.astype(BF16),
                       norm_ffn.reshape(1, d), w_q.T.astype(BF16), keys)
    g1, n1, g2, r2 = _topk(st)
    return _peer(xnt, expert_u.astype(BF16), expert_v.T.astype(BF16), g1, n1, g2, r2, x1, out_norm)


def kernel(x, norm_mix, w_in, conv_w, conv_b, w_branch_a, w_branch_b, w_out, norm_ffn, w_q, sub_keys,
           expert_u, expert_v, final_norm):
    batch, seq, d = x.shape
    depth = w_in.shape[0]
    assert depth == 1, "the final RMSNorm is fused into the last layer's PEER kernel"
    x2 = x.reshape(batch * seq, d)
    out = _layer(x2, batch, seq, norm_mix[0], w_in[0], conv_w[0], conv_b[0], w_branch_a[0],
                 w_branch_b[0], w_out[0], norm_ffn[0], w_q[0], sub_keys[0], expert_u[0], expert_v[0],
                 final_norm.reshape(1, d))
    return out.reshape(batch, seq, d)
```

```python
import functools
import math

import jax
import jax.numpy as jnp
from jax import lax
from jax.experimental import pallas as pl
from jax.experimental.pallas import tpu as pltpu

F32 = jnp.float32
BF16 = jnp.bfloat16

RMS_EPS = 1e-6
SB_HEADS = 8
SB_HEAD_DIM = 64
SB_WIDTH = SB_HEADS * SB_HEAD_DIM
CONV_K = 3
PEER_HEADS = 8
N_KEYS = 128
PEER_TOPK = 16
LANES = 128
SUBLANES = 8
BF16_ROWS = 16
VMEM_LIMIT_BYTES = 56 * 1024 * 1024
EXP_UNDERFLOW_LOG = -110.0

IN_TM = 512
ATT_TQ = 256
ATT_TK = 256
ATT_WIDTH = 256
MID_TM = 512
TOPK_TN = 128
PEER_TN = 512
PEER_TE = 2048
PEER_LANE_CHUNK = 256


def _cparams(sem):
    return pltpu.CompilerParams(dimension_semantics=sem, vmem_limit_bytes=VMEM_LIMIT_BYTES)


def _inproj_kernel(x_ref, nm_ref, w_ref, cw_ref, cb_ref, qkv_ref, cbo_ref, gate_ref, carry_ref,
                   *, tiles_per_seq, width):
    i = pl.program_id(0)
    x = x_ref[...]
    ms = jnp.mean(x * x, axis=-1, keepdims=True)
    h = (x * lax.rsqrt(ms + RMS_EPS) * nm_ref[...]).astype(BF16)
    o3 = 3 * width
    qkv_ref[...] = jnp.dot(h, w_ref[:, 0:o3], preferred_element_type=F32).astype(BF16)

    c = jnp.dot(h, w_ref[:, o3:2 * o3], preferred_element_type=F32)
    cb, cc, cu = c[:, 0:width], c[:, width:2 * width], c[:, 2 * width:3 * width]
    z = cc * cu
    tm = z.shape[0]

    @pl.when(i % tiles_per_seq == 0)
    def _():
        carry_ref[...] = jnp.zeros_like(carry_ref)

    prev = carry_ref[...]
    rows = lax.broadcasted_iota(jnp.int32, z.shape, 0)
    z1 = jnp.where(rows == 0, prev[7:8, :], pltpu.roll(z, 1, 0))
    z2 = jnp.where(rows == 0, prev[6:7, :], jnp.where(rows == 1, prev[7:8, :], pltpu.roll(z, 2, 0)))
    y = cw_ref[0:1, :] * z2 + cw_ref[1:2, :] * z1 + cw_ref[2:3, :] * z + cb_ref[...]
    cbo_ref[...] = (cb * y).astype(BF16)
    carry_ref[...] = z[tm - SUBLANES:, :]

    g = jnp.dot(h, w_ref[:, 2 * o3:], preferred_element_type=F32)
    gate_ref[...] = 0.5 * jnp.tanh(0.5 * g) + 0.5


def _inproj(x2, norm_mix, w_in, conv_w, conv_b, seq):
    n, d = x2.shape
    width = SB_WIDTH
    in_width = w_in.shape[1]
    tm = IN_TM
    kern = functools.partial(_inproj_kernel, tiles_per_seq=seq // tm, width=width)
    return pl.pallas_call(
        kern,
        grid=(n // tm,),
        in_specs=[
            pl.BlockSpec((tm, d), lambda i: (i, 0)),
            pl.BlockSpec((1, d), lambda i: (0, 0)),
            pl.BlockSpec((d, in_width), lambda i: (0, 0), pipeline_mode=pl.Buffered(1)),
            pl.BlockSpec((CONV_K, width), lambda i: (0, 0)),
            pl.BlockSpec((1, width), lambda i: (0, 0)),
        ],
        out_specs=[
            pl.BlockSpec((tm, 3 * width), lambda i: (i, 0)),
            pl.BlockSpec((tm, width), lambda i: (i, 0)),
            pl.BlockSpec((tm, 2 * d), lambda i: (i, 0)),
        ],
        out_shape=[
            jax.ShapeDtypeStruct((n, 3 * width), BF16),
            jax.ShapeDtypeStruct((n, width), BF16),
            jax.ShapeDtypeStruct((n, 2 * d), F32),
        ],
        scratch_shapes=[pltpu.VMEM((SUBLANES, width), F32)],
        compiler_params=_cparams(("arbitrary",)),
        name="inproj",
    )(x2, norm_mix, w_in, conv_w, conv_b)


def _att_weights(qh, kj, tri, c, mask):
    z = lax.dot_general(qh, kj, (((1,), (1,)), ((), ())), preferred_element_type=F32)
    sp = jnp.maximum(z, 0.0) + jnp.log(1.0 + jnp.exp(-jnp.abs(z)))
    l1mb = -sp
    if mask is not None:
        l1mb = jnp.where(mask, l1mb, 0.0)
    hi = l1mb.astype(BF16)
    lo = (l1mb - hi.astype(F32)).astype(BF16)
    excl = (jnp.dot(hi, tri, preferred_element_type=F32)
            + jnp.dot(lo, tri, preferred_element_type=F32))
    a = jnp.exp((z - sp) + excl + c)
    if mask is not None:
        a = jnp.where(mask, a, 0.0)
    return a.astype(BF16), jnp.sum(l1mb, axis=-1, keepdims=True)


def _attention_kernel(q_ref, k_ref, v_ref, o_ref, *, tq, tk, scale):
    i = pl.program_id(2)
    width = q_ref.shape[1]
    heads = width // SB_HEAD_DIM
    q = q_ref[...] * jnp.asarray(scale, BF16)
    kr = lax.broadcasted_iota(jnp.int32, (tk, tk), 0)
    kc = lax.broadcasted_iota(jnp.int32, (tk, tk), 1)
    tri = jnp.where(kr > kc, 1.0, 0.0).astype(BF16)
    qr = lax.broadcasted_iota(jnp.int32, (tq, tk), 0)
    qc = lax.broadcasted_iota(jnp.int32, (tq, tk), 1)
    diag_mask = qc < qr

    head_of_lane = lax.broadcasted_iota(jnp.int32, (tq, width), 1) // SB_HEAD_DIM
    zero = jnp.zeros_like(q)
    qs = [jnp.where(head_of_lane == h, q, zero) for h in range(heads)]

    def all_heads(j, cs, o, mask):
        s = pl.multiple_of(j * tk, tk)
        kj = k_ref[pl.ds(s, tk), :]
        vj = v_ref[pl.ds(s, tk), :]
        weights, totals = zip(*[_att_weights(qs[h], kj, tri, cs[h], mask) for h in range(heads)])
        v_heads = jnp.concatenate([jnp.where(head_of_lane == h, vj, zero) for h in range(heads)], axis=0)
        o = o + jnp.dot(jnp.concatenate(weights, axis=1), v_heads, preferred_element_type=F32)
        return tuple(c + t for c, t in zip(cs, totals)), o

    def alive(cs):
        return (jnp.max(functools.reduce(jnp.maximum, cs)) >= EXP_UNDERFLOW_LOG).astype(jnp.int32)

    c0 = (jnp.zeros((tq, 1), F32),) * heads
    o0 = jnp.zeros((tq, width), F32)

    @pl.when(i == 0)
    def _():
        _, o = all_heads(i, c0, o0, diag_mask)
        o_ref[...] = o.astype(o_ref.dtype)

    @pl.when(i > 0)
    def _():
        cs, o = all_heads(i, c0, o0, diag_mask)
        cs, o = all_heads(i - 1, cs, o, None)

        def cond(state):
            jj, _, _, live = state
            return jnp.logical_and(jj <= i, live > 0)

        def body(state):
            jj, cs, o, _ = state
            cs, o = all_heads(i - jj, cs, o, None)
            return jj + 1, cs, o, alive(cs)

        _, _, o, _ = lax.while_loop(cond, body, (jnp.int32(2), cs, o, alive(cs)))
        o_ref[...] = o.astype(o_ref.dtype)


def _attention(qkv, batch, seq):
    n = qkv.shape[0]
    tq, tk = ATT_TQ, ATT_TK
    assert tq == tk
    width = ATT_WIDTH
    groups = SB_WIDTH // width
    nq = seq // tq
    kern = functools.partial(_attention_kernel, tq=tq, tk=tk, scale=SB_HEAD_DIM ** -0.5)
    return pl.pallas_call(
        kern,
        grid=(batch, groups, nq),
        in_specs=[
            pl.BlockSpec((tq, width), lambda b, p, i: (b * nq + i, p)),
            pl.BlockSpec((seq, width), lambda b, p, i: (b, groups + p)),
            pl.BlockSpec((seq, width), lambda b, p, i: (b, 2 * groups + p)),
        ],
        out_specs=pl.BlockSpec((tq, width), lambda b, p, i: (b * nq + i, p)),
        out_shape=jax.ShapeDtypeStruct((n, SB_WIDTH), BF16),
        compiler_params=_cparams(("arbitrary", "arbitrary", "arbitrary")),
        name="stickbreak_attention",
    )(qkv, qkv, qkv)


def _mid_kernel(att_ref, cbo_ref, gate_ref, x_ref, wa_ref, wb_ref, wo_ref, nf_ref, wqt_ref, keys_ref,
                x1_ref, xnt_ref, st_ref):
    d = x_ref.shape[1]
    ya = jnp.dot(att_ref[...], wa_ref[...], preferred_element_type=F32)
    yb = jnp.dot(cbo_ref[...], wb_ref[...], preferred_element_type=F32)
    merged = (gate_ref[:, 0:d] * ya + gate_ref[:, d:2 * d] * yb).astype(BF16)
    x1 = x_ref[...] + jnp.dot(merged, wo_ref[...], preferred_element_type=F32)
    x1_ref[...] = x1
    ms = jnp.mean(x1 * x1, axis=-1, keepdims=True)
    xn = x1 * lax.rsqrt(ms + RMS_EPS) * nf_ref[...]
    xnt_ref[...] = xn.T.astype(BF16)
    qt = lax.dot_general(wqt_ref[...], xn.astype(BF16), (((1,), (1,)), ((), ())),
                         preferred_element_type=F32)
    for hp in range(st_ref.shape[0]):
        qhp = qt[hp * N_KEYS:(hp + 1) * N_KEYS, :].astype(BF16)
        st_ref[hp] = jnp.dot(keys_ref[hp], qhp, preferred_element_type=F32)


def _mid(att, cbo, gates, x2, wa, wb, wo, norm_ffn, wqt, keys):
    n, d = x2.shape
    tm = MID_TM
    nhp = keys.shape[0]
    const2 = lambda i: (0, 0)
    return pl.pallas_call(
        _mid_kernel,
        grid=(n // tm,),
        in_specs=[
            pl.BlockSpec((tm, SB_WIDTH), lambda i: (i, 0)),
            pl.BlockSpec((tm, SB_WIDTH), lambda i: (i, 0)),
            pl.BlockSpec((tm, 2 * d), lambda i: (i, 0)),
            pl.BlockSpec((tm, d), lambda i: (i, 0)),
            pl.BlockSpec(wa.shape, const2, pipeline_mode=pl.Buffered(1)),
            pl.BlockSpec(wb.shape, const2, pipeline_mode=pl.Buffered(1)),
            pl.BlockSpec(wo.shape, const2, pipeline_mode=pl.Buffered(1)),
            pl.BlockSpec((1, d), const2),
            pl.BlockSpec(wqt.shape, const2, pipeline_mode=pl.Buffered(1)),
            pl.BlockSpec(keys.shape, lambda i: (0, 0, 0), pipeline_mode=pl.Buffered(1)),
        ],
        out_specs=[
            pl.BlockSpec((tm, d), lambda i: (i, 0)),
            pl.BlockSpec((d, tm), lambda i: (0, i)),
            pl.BlockSpec((nhp, N_KEYS, tm), lambda i: (0, 0, i)),
        ],
        out_shape=[
            jax.ShapeDtypeStruct((n, d), F32),
            jax.ShapeDtypeStruct((d, n), BF16),
            jax.ShapeDtypeStruct((nhp, N_KEYS, n), F32),
        ],
        compiler_params=_cparams(("arbitrary",)),
        name="mid_proj",
    )(att, cbo, gates, x2, wa, wb, wo, norm_ffn, wqt, keys)


def _candidate_cells():
    return [(a, b) for a in range(PEER_TOPK) for b in range(PEER_TOPK) if (a + 1) * (b + 1) <= PEER_TOPK]


def _topk_kernel(st_ref, g1_ref, n1_ref, g2_ref, r2_ref, rank_scr, tops_scr):
    k = PEER_TOPK
    neg_inf = jnp.asarray(-jnp.inf, F32)
    kio = lax.broadcasted_iota(jnp.int32, (N_KEYS, LANES), 0).astype(F32)
    sub = lax.broadcasted_iota(jnp.int32, (SUBLANES, LANES), 0)

    def record_top(p, h, r, m):
        slot = p * k + r
        tops_scr[slot] = jnp.where(sub == h, m, tops_scr[slot])

    def extract_all_equal(h2, tied):
        for hp in range(4):
            h = 2 * h2 + hp // 2
            p = hp % 2
            x = st_ref[2 * h + p]
            peeled = []
            for r in range(k):
                m = jnp.max(x, axis=0, keepdims=True)
                x = jnp.where(x == m, neg_inf, x)
                record_top(p, h, r, m)
                peeled.append(m)
            x = st_ref[2 * h + p]
            rank = jnp.full((N_KEYS, LANES), float(k), F32)
            for r in range(k):
                rank = jnp.where(x == peeled[r], float(r), rank)
            rank_scr[2 * h + p] = rank
            removed = jnp.sum(jnp.where(rank < float(k), 1.0, 0.0), axis=0, keepdims=True)
            tied = jnp.maximum(tied, jnp.where(removed != float(k), 1.0, 0.0))
        return tied

    tied = lax.fori_loop(0, PEER_HEADS // 2, extract_all_equal, jnp.zeros((1, LANES), F32))

    @pl.when(jnp.max(tied) > 0.0)
    def _():
        def extract_first(hp, _):
            x = st_ref[hp]
            rank = jnp.full((N_KEYS, LANES), float(k), F32)
            h = hp // 2
            p = hp % 2
            for r in range(k):
                m = jnp.max(x, axis=0, keepdims=True)
                first = jnp.min(jnp.where(x == m, kio, float(N_KEYS)), axis=0, keepdims=True)
                hit = kio == first
                rank = jnp.where(hit, float(r), rank)
                x = jnp.where(hit, neg_inf, x)
                record_top(p, h, r, m)
            rank_scr[hp] = rank
            return 0

        lax.fori_loop(0, st_ref.shape[0], extract_first, 0)

    t1 = [tops_scr[a] for a in range(k)]
    t2 = [tops_scr[k + b] for b in range(k)]
    cells = _candidate_cells()
    val = {ab: t1[ab[0]] + t2[ab[1]] for ab in cells}
    beat = {ab: jnp.full((SUBLANES, LANES), float((ab[0] + 1) * (ab[1] + 1) - 1), F32) for ab in cells}
    for ip, p_ in enumerate(cells):
        for q_ in cells[ip + 1:]:
            if p_[0] <= q_[0] and p_[1] <= q_[1]:
                continue
            t = jnp.where(val[p_] >= val[q_], 1.0, 0.0)
            beat[q_] = beat[q_] + t
            beat[p_] = beat[p_] + (1.0 - t)
    top = val[(0, 0)]
    zsum = jnp.zeros((SUBLANES, LANES), F32)
    n_sel = [jnp.zeros((SUBLANES, LANES), F32) for _ in range(k)]
    for ab in cells:
        sel = beat[ab] < float(k)
        zsum = zsum + jnp.where(sel, jnp.exp(val[ab] - top), 0.0)
        n_sel[ab[0]] = n_sel[ab[0]] + jnp.where(sel, 1.0, 0.0)
    zinv = 0.5 / zsum

    for h in range(PEER_HEADS):
        rank1 = rank_scr[2 * h]
        rank2 = rank_scr[2 * h + 1]
        m1 = t1[0][h:h + 1, :]
        m2 = t2[0][h:h + 1, :]
        g1_ref[h] = jnp.where(rank1 < float(k), jnp.exp(st_ref[2 * h] - m1), 0.0)
        g2 = jnp.where(rank2 < float(k), jnp.exp(st_ref[2 * h + 1] - m2) * zinv[h:h + 1, :], 0.0)
        g2_ref[h] = g2.astype(BF16)
        r2_ref[h] = rank2.astype(BF16)
        n1 = jnp.zeros((N_KEYS, LANES), F32)
        for a in range(k):
            n1 = jnp.where(rank1 == float(a), n_sel[a][h:h + 1, :], n1)
        n1_ref[h] = n1


def _topk(st):
    nhp, nk, n = st.shape
    tn = TOPK_TN
    heads = nhp // 2
    spec = pl.BlockSpec((heads, nk, tn), lambda i: (0, 0, i))
    return pl.pallas_call(
        _topk_kernel,
        grid=(n // tn,),
        in_specs=[pl.BlockSpec((nhp, nk, tn), lambda i: (0, 0, i))],
        out_specs=[spec, spec, spec, spec],
        out_shape=[
            jax.ShapeDtypeStruct((heads, nk, n), F32),
            jax.ShapeDtypeStruct((heads, nk, n), F32),
            jax.ShapeDtypeStruct((heads, nk, n), BF16),
            jax.ShapeDtypeStruct((heads, nk, n), BF16),
        ],
        scratch_shapes=[
            pltpu.VMEM((nhp, nk, tn), F32),
            pltpu.VMEM((2 * PEER_TOPK, SUBLANES, tn), F32),
        ],
        compiler_params=_cparams(("arbitrary",)),
        name="peer_topk",
    )(st)


def _peer_tick(xnt_ref, u_ref, vt_ref, g1_ref, n1_ref, g2_ref, r2_ref, acc_scr, p_scr,
               s_write, s_read, first_of_token_tile):
    te, tn = s_read.shape
    s_write[...] = jnp.dot(u_ref[...], xnt_ref[...], preferred_element_type=F32)

    group = (BF16_ROWS, PEER_LANE_CHUNK)
    for j in range(te // N_KEYS):
        rows = slice(j * N_KEYS, (j + 1) * N_KEYS)
        for lc in range(tn // PEER_LANE_CHUNK):
            cols = slice(lc * PEER_LANE_CHUNK, (lc + 1) * PEER_LANE_CHUNK)
            w = jnp.zeros((N_KEYS // BF16_ROWS,) + group, BF16)
            for h in range(PEER_HEADS):
                g1 = jnp.broadcast_to(g1_ref[h, j:j + 1, cols], group).astype(BF16)
                n1 = jnp.broadcast_to(n1_ref[h, j:j + 1, cols], group).astype(BF16)
                w = w + jnp.where(r2_ref[h, :, :, cols] < n1, g2_ref[h, :, :, cols] * g1, jnp.zeros_like(w))
            s = s_read[rows, cols]
            act = s * (1.0 + lax.erf(s * math.sqrt(0.5)))
            p_scr[rows, cols] = act.astype(BF16) * w.reshape(N_KEYS, PEER_LANE_CHUNK)

    upd = jnp.dot(vt_ref[...], p_scr[...], preferred_element_type=F32)
    acc_scr[...] = jnp.where(first_of_token_tile, upd, acc_scr[...] + upd)


def _peer_kernel(xnt_ref, u_ref, vt_ref, g1_ref, n1_ref, g2_ref, r2_ref, x1_ref, fn_ref, out_ref,
                 s0, s1, p_scr, acc_scr, *, n_items, e_tiles):
    k = pl.program_id(0)
    e_b = jnp.clip(k - 1, 0, n_items - 1) % e_tiles

    @pl.when(k == 0)
    def _():
        s1[...] = jnp.zeros_like(s1)
        acc_scr[...] = jnp.zeros_like(acc_scr)

    refs = (xnt_ref, u_ref, vt_ref, g1_ref, n1_ref, g2_ref, r2_ref, acc_scr, p_scr)

    @pl.when(k % 2 == 0)
    def _():
        _peer_tick(*refs, s0, s1, e_b == 0)

    @pl.when(k % 2 == 1)
    def _():
        _peer_tick(*refs, s1, s0, e_b == 0)

    @pl.when(jnp.logical_and(k >= 1, e_b == e_tiles - 1))
    def _():
        y = x1_ref[...] + acc_scr[...].T
        ms = jnp.mean(y * y, axis=-1, keepdims=True)
        out_ref[...] = y * lax.rsqrt(ms + RMS_EPS) * fn_ref[...]


def _peer(xnt, u, vt, g1, n1, g2, r2, x1, final_norm):
    d, n = xnt.shape
    n_exp = u.shape[0]
    tn, te = PEER_TN, PEER_TE
    heads = g1.shape[0]
    i1_per_tile = te // N_KEYS
    assert i1_per_tile % SUBLANES == 0
    row_groups = N_KEYS // BF16_ROWS
    g2 = g2.reshape(heads, row_groups, BF16_ROWS, n)
    r2 = r2.reshape(heads, row_groups, BF16_ROWS, n)
    e_tiles = n_exp // te
    n_items = (n // tn) * e_tiles

    def item(k, lag):
        it = jnp.clip(k - lag, 0, n_items - 1)
        return it // e_tiles, it % e_tiles

    kern = functools.partial(_peer_kernel, n_items=n_items, e_tiles=e_tiles)
    return pl.pallas_call(
        kern,
        grid=(n_items + 1,),
        in_specs=[
            pl.BlockSpec((d, tn), lambda k: (0, item(k, 0)[0])),
            pl.BlockSpec((te, d), lambda k: (item(k, 0)[1], 0)),
            pl.BlockSpec((d, te), lambda k: (0, item(k, 1)[1])),
            pl.BlockSpec((heads, i1_per_tile, tn), lambda k: (0, item(k, 1)[1], item(k, 1)[0])),
            pl.BlockSpec((heads, i1_per_tile, tn), lambda k: (0, item(k, 1)[1], item(k, 1)[0])),
            pl.BlockSpec((heads, row_groups, BF16_ROWS, tn), lambda k: (0, 0, 0, item(k, 1)[0])),
            pl.BlockSpec((heads, row_groups, BF16_ROWS, tn), lambda k: (0, 0, 0, item(k, 1)[0])),
            pl.BlockSpec((tn, d), lambda k: (item(k, 1)[0], 0)),
            pl.BlockSpec((1, d), lambda k: (0, 0)),
        ],
        out_specs=pl.BlockSpec((tn, d), lambda k: (item(k, 1)[0], 0)),
        out_shape=jax.ShapeDtypeStruct((n, d), F32),
        scratch_shapes=[
            pltpu.VMEM((te, tn), F32),
            pltpu.VMEM((te, tn), F32),
            pltpu.VMEM((te, tn), BF16),
            pltpu.VMEM((d, tn), F32),
        ],
        compiler_params=_cparams(("arbitrary",)),
        name="peer_dense",
    )(xnt, u, vt, g1, n1, g2, r2, x1, final_norm)


def _layer(x2, batch, seq, norm_mix, w_in, conv_w, conv_b, w_a, w_b, w_o, norm_ffn, w_q, sub_keys,
           expert_u, expert_v, out_norm):
    d = x2.shape[1]
    qkv, cbo, gates = _inproj(x2, norm_mix.reshape(1, d), w_in.astype(BF16),
                              conv_w.reshape(CONV_K, -1), conv_b.reshape(1, -1), seq)
    att = _attention(qkv, batch, seq)
    keys = sub_keys.reshape(-1, N_KEYS, sub_keys.shape[-1]).astype(BF16)
    x1, xnt, st = _mid(att, cbo, gates, x2, w_a.astype(BF16), w_b.astype(BF16), w_o.astype(BF16),
                       norm_ffn.reshape(1, d), w_q.T.astype(BF16), keys)
    g1, n1, g2, r2 = _topk(st)
    return _peer(xnt, expert_u.astype(BF16), expert_v.T.astype(BF16), g1, n1, g2, r2, x1, out_norm)


def kernel(x, norm_mix, w_in, conv_w, conv_b, w_branch_a, w_branch_b, w_out, norm_ffn, w_q, sub_keys,
           expert_u, expert_v, final_norm):
    batch, seq, d = x.shape
    depth = w_in.shape[0]
    assert depth == 1, "the final RMSNorm is fused into the last layer's PEER kernel"
    x2 = x.reshape(batch * seq, d)
    out = _layer(x2, batch, seq, norm_mix[0], w_in[0], conv_w[0], conv_b[0], w_branch_a[0],
                 w_branch_b[0], w_out[0], norm_ffn[0], w_q[0], sub_keys[0], expert_u[0], expert_v[0],
                 final_norm.reshape(1, d))
    return out.reshape(batch, seq, d)
```
